```python
import math
import jax
import jax.numpy as jnp
from jax import lax
import numpy as np

D_MODEL = 4096
BATCH = 1
SEQ = 8192
DEPTH = 1
DEC_BATCH = 32
DEC_SEQ = 8
PAST_LEN = 8192
PAGE_SIZE = 128

DN_HEADS = D_MODEL // 256
DN_HEAD_DIM = 128
DN_WIDTH = DN_HEADS * DN_HEAD_DIM
DN_CONV = 4
DN_CHUNK = 64
ATT_GROUPS = ((128, 1), (512, 4), (2048, 16))
N_GROUPS = 3
ATT_HEADS_PER_GROUP = D_MODEL // 512
ATT_HEAD_DIM = 128
ATT_GROUP_WIDTH = ATT_HEADS_PER_GROUP * ATT_HEAD_DIM
ROPE_THETA = 10000.0
N_IN = 4 * DN_WIDTH + 2 * DN_HEADS + 3 * N_GROUPS * ATT_GROUP_WIDTH + 2 * D_MODEL
D_FF = ((8 * D_MODEL // 3 + 255) // 256) * 256
FFN_CONV = 3
EPS = 1e-6

kernel_name = 'hybrid_gdn_dilated_swa_convffn_step'


def rms_norm(x, gain):
    xf = x.astype(jnp.float32)
    y = xf * lax.rsqrt(jnp.mean(xf * xf, axis=-1, keepdims=True) + EPS)
    return (y * gain.astype(jnp.float32)).astype(x.dtype)


def l2_normalize(t):
    return t * lax.rsqrt(jnp.sum(t * t, axis=-1, keepdims=True) + EPS)


def causal_dwconv(x, prev, w):
    k_width = w.shape[0]
    xp = jnp.concatenate([prev.astype(x.dtype), x], axis=1)
    y = lax.conv_general_dilated(
        xp, w.astype(x.dtype)[:, None, :], window_strides=(1,), padding='VALID',
        dimension_numbers=('NWC', 'WIO', 'NWC'), feature_group_count=x.shape[-1])
    return y, xp[:, xp.shape[1] - (k_width - 1):]


def rope(x, pos):
    half = x.shape[-1] // 2
    inv_freq = ROPE_THETA ** (-jnp.arange(half, dtype=jnp.float32) / half)
    ang = pos.astype(jnp.float32)[:, None] * inv_freq[None, :]
    cos = jnp.cos(ang)[None, :, None, :]
    sin = jnp.sin(ang)[None, :, None, :]
    xf = x.astype(jnp.float32)
    x1, x2 = xf[..., :half], xf[..., half:]
    return jnp.concatenate([x1 * cos - x2 * sin, x2 * cos + x1 * sin], axis=-1).astype(x.dtype)


def _softmax_lse(s):
    m = jnp.max(s, axis=-1, keepdims=True)
    p = jnp.exp(s - m)
    l = jnp.sum(p, axis=-1, keepdims=True)
    return p / l, (m + jnp.log(l))[..., 0]


def dilated_attn_prompt(q, k, v, window, dil):
    bsz, s_len, n_h, d_h = q.shape
    nb = window // dil
    period = nb * dil
    s_pad = -(-s_len // period) * period
    m_len = s_pad // dil
    n_blk = m_len // nb

    def to_blocks(t):
        t = jnp.pad(t, ((0, 0), (0, s_pad - s_len), (0, 0), (0, 0)))
        t = t.reshape(bsz, m_len, dil, n_h, d_h).transpose(0, 2, 1, 3, 4)
        return t.reshape(bsz, dil, n_blk, nb, n_h, d_h)

    def with_prev(t):
        prev = jnp.pad(t, ((0, 0), (0, 0), (1, 0), (0, 0), (0, 0), (0, 0)))[:, :, :-1]
        return jnp.concatenate([prev, t], axis=3)

    qb = to_blocks(q)
    kk = with_prev(to_blocks(k))
    vv = with_prev(to_blocks(v))
    s = jnp.einsum('brnqhd,brnkhd->brnhqk', qb, kk,
                   preferred_element_type=jnp.float32) * (d_h ** -0.5)
    i = jnp.arange(nb)[:, None]
    j = jnp.arange(2 * nb)[None, :]
    band = (j >= i) & (j <= i + nb)
    not_before_start = (jnp.arange(n_blk)[:, None, None] > 0) | (j[None] >= nb)
    mask = band[None] & not_before_start
    s = jnp.where(mask[None, None, :, None], s, -jnp.inf)
    p, lse = _softmax_lse(s)
    o = jnp.einsum('brnhqk,brnkhd->brnqhd', p, vv.astype(jnp.float32))
    o = o.reshape(bsz, dil, m_len, n_h, d_h).transpose(0, 2, 1, 3, 4).reshape(bsz, s_pad, n_h, d_h)
    lse = lse.transpose(0, 1, 2, 4, 3).reshape(bsz, dil, m_len, n_h).transpose(0, 2, 1, 3)
    lse = lse.reshape(bsz, s_pad, n_h)
    return o[:, :s_len], lse[:, :s_len]


def dilated_attn_sample(q, kc, vc, window, dil):
    t_len = q.shape[1]
    buf_len = kc.shape[1] - t_len
    nb = window // dil
    idx = buf_len + jnp.arange(t_len)[:, None] - dil * jnp.arange(nb + 1)[None, :]
    valid = idx >= 0
    idx = jnp.maximum(idx, 0)
    kg = kc[:, idx]
    vg = vc[:, idx]
    s = jnp.einsum('bthd,btnhd->bthn', q, kg,
                   preferred_element_type=jnp.float32) * (q.shape[-1] ** -0.5)
    s = jnp.where(valid[None, :, None, :], s, -jnp.inf)
    p, lse = _softmax_lse(s)
    o = jnp.einsum('bthn,btnhd->bthd', p, vg.astype(jnp.float32))
    return o, lse


def merge_groups(outs, lses):
    w = jax.nn.softmax(jnp.stack(lses, axis=0), axis=0)
    return jnp.einsum('gbth,gbthd->bthd', w, jnp.stack(outs, axis=0))


def gated_delta_rule(q, k, v, beta, g, state):
    f32 = jnp.float32
    bsz, t_len, n_h, _ = q.shape
    c = DN_CHUNK
    t_pad = -(-t_len // c) * c
    n_c = t_pad // c

    def pad(t):
        return jnp.pad(t.astype(f32), [(0, 0), (0, t_pad - t_len)] + [(0, 0)] * (t.ndim - 2))

    def chunks(t):
        return t.reshape(bsz, n_c, c, n_h, -1).transpose(1, 0, 3, 2, 4)

    qc, kc, vc = chunks(pad(q)), chunks(pad(k)), chunks(pad(v))
    bc = chunks(pad(beta)[..., None])[..., 0]
    gc = chunks(pad(g)[..., None])[..., 0]
    big_g = jnp.cumsum(gc, axis=-1)
    causal = jnp.tril(jnp.ones((c, c), dtype=bool))
    strict = jnp.tril(jnp.ones((c, c), dtype=bool), -1)
    gamma = jnp.exp(jnp.where(causal, big_g[..., :, None] - big_g[..., None, :], -jnp.inf))
    a_mat = jnp.where(strict, bc[..., :, None] * jnp.einsum('...id,...jd->...ij', kc, kc) * gamma, 0.0)
    rhs = jnp.concatenate([vc * bc[..., None], kc * (bc * jnp.exp(big_g))[..., None]], axis=-1)
    sol = lax.linalg.triangular_solve(a_mat, rhs, left_side=True, lower=True, unit_diagonal=True)
    dv = vc.shape[-1]
    u_c, w_c = sol[..., :dv], sol[..., dv:]
    a_qk = jnp.einsum('...id,...jd->...ij', qc, kc) * gamma
    q_dec = qc * jnp.exp(big_g)[..., None]
    k_dec = kc * jnp.exp(big_g[..., -1:] - big_g)[..., None]
    g_last = jnp.exp(big_g[..., -1])

    def step(s, xs):
        qd, kd, u, w, aqk, gl = xs
        v_new = u - jnp.einsum('bhcd,bhde->bhce', w, s)
        o = jnp.einsum('bhcd,bhde->bhce', qd, s) + jnp.einsum('bhij,bhje->bhie', aqk, v_new)
        s = s * gl[..., None, None] + jnp.einsum('bhcd,bhce->bhde', kd, v_new)
        return s, o

    s_fin, o = lax.scan(step, state.astype(f32), (q_dec, k_dec, u_c, w_c, a_qk, g_last))
    o = o.transpose(1, 0, 3, 2, 4).reshape(bsz, t_pad, n_h, dv)[:, :t_len]
    return o, s_fin


def _in_splits():
    sizes = (3 * DN_WIDTH, DN_WIDTH, DN_HEADS, DN_HEADS, 3 * N_GROUPS * ATT_GROUP_WIDTH, D_MODEL)
    out, acc = [], 0
    for s in sizes:
        acc += s
        out.append(acc)
    return out


def decoder_layer(x, pos, dn_conv_prev, dn_state, ffn_conv_prev, att_bufs,
                  norm_mix, w_in, dn_conv_w, dn_a_log, dn_dt_bias, dn_out_norm,
                  w_branch_dn, w_branch_att, w_out, norm_ffn, w_ffn_gate, w_ffn_up,
                  ffn_conv_w, w_ffn_down):
    f32 = jnp.float32
    bsz, t_len, _ = x.shape
    n = rms_norm(x, norm_mix)
    proj = n @ w_in
    qkv_dn, z_dn, b_dn, a_dn, att, g_dn, g_att = jnp.split(proj, _in_splits(), axis=-1)

    qkv_dn, dn_conv_new = causal_dwconv(qkv_dn, dn_conv_prev, dn_conv_w)
    qkv_dn = jax.nn.silu(qkv_dn.astype(f32)).reshape(bsz, t_len, 3, DN_HEADS, DN_HEAD_DIM)
    q_dn = l2_normalize(qkv_dn[:, :, 0]) * (DN_HEAD_DIM ** -0.5)
    k_dn = l2_normalize(qkv_dn[:, :, 1])
    v_dn = qkv_dn[:, :, 2]
    beta = jax.nn.sigmoid(b_dn.astype(f32))
    log_decay = -jnp.exp(dn_a_log.astype(f32)) * jax.nn.softplus(a_dn.astype(f32) + dn_dt_bias.astype(f32))
    o_dn, dn_state_new = gated_delta_rule(q_dn, k_dn, v_dn, beta, log_decay, dn_state)
    o_dn = rms_norm(o_dn, dn_out_norm) * jax.nn.silu(z_dn.astype(f32)).reshape(bsz, t_len, DN_HEADS, DN_HEAD_DIM)
    o_dn = o_dn.reshape(bsz, t_len, DN_WIDTH).astype(x.dtype)

    att = att.reshape(bsz, t_len, N_GROUPS, 3, ATT_HEADS_PER_GROUP, ATT_HEAD_DIM)
    outs, lses, kv_new = [], [], []
    for gi, (win, dil) in enumerate(ATT_GROUPS):
        q = rope(att[:, :, gi, 0], pos)
        k = rope(att[:, :, gi, 1], pos)
        v = att[:, :, gi, 2]
        if att_bufs is None:
            o, lse = dilated_attn_prompt(q, k, v, win, dil)
            keep = min(win, t_len)
            kv_new.append(jnp.stack([k[:, t_len - keep:], v[:, t_len - keep:]], axis=2))
        else:
            buf = att_bufs[gi]
            kc = jnp.concatenate([buf[:, :, 0].astype(k.dtype), k], axis=1)
            vc = jnp.concatenate([buf[:, :, 1].astype(v.dtype), v], axis=1)
            o, lse = dilated_attn_sample(q, kc, vc, win, dil)
            kv_new.append(jnp.stack([k, v], axis=2))
        outs.append(o)
        lses.append(lse)
    o_att = merge_groups(outs, lses).reshape(bsz, t_len, ATT_GROUP_WIDTH).astype(x.dtype)

    mix = jax.nn.sigmoid(g_dn) * (o_dn @ w_branch_dn) + jax.nn.sigmoid(g_att) * (o_att @ w_branch_att)
    x = x + mix @ w_out

    n = rms_norm(x, norm_ffn)
    gate_pre, ffn_conv_new = causal_dwconv(n @ w_ffn_gate, ffn_conv_prev, ffn_conv_w)
    x = x + (jax.nn.silu(gate_pre) * (n @ w_ffn_up)) @ w_ffn_down
    return x, (kv_new[0], kv_new[1], kv_new[2], dn_state_new, dn_conv_new, ffn_conv_new)


def stack_layers(states):
    return tuple(jnp.stack(arrs, axis=0) for arrs in zip(*states))


def setup_inputs(seed: int = 0) -> dict:
    key = jax.random.key(seed)
    ks = jax.random.split(key, 24)
    f32 = jnp.float32

    def normal(k, shape, scale):
        return jax.random.normal(k, shape, f32) * scale

    def gain(k, shape):
        return 1.0 + 0.02 * jax.random.normal(k, shape, f32)

    def kv_shape(w):
        return (DEPTH, DEC_BATCH, min(w, PAST_LEN), 2, ATT_HEADS_PER_GROUP, ATT_HEAD_DIM)

    dt = jnp.exp(jax.random.uniform(ks[11], (DEPTH, DN_HEADS), f32, math.log(1e-3), math.log(1e-1)))
    return {
        'x_prompt': normal(ks[0], (BATCH, SEQ, D_MODEL), 1.0),
        'x_sample': normal(ks[1], (DEC_BATCH, DEC_SEQ, D_MODEL), 1.0),
        'cache_kv_w128': normal(ks[2], kv_shape(ATT_GROUPS[0][0]), 1.0),
        'cache_kv_w512': normal(ks[3], kv_shape(ATT_GROUPS[1][0]), 1.0),
        'cache_kv_w2048': normal(ks[4], kv_shape(ATT_GROUPS[2][0]), 1.0),
        'state_dn': normal(ks[5], (DEPTH, DEC_BATCH, DN_HEADS, DN_HEAD_DIM, DN_HEAD_DIM), 0.1),
        'state_dn_conv': normal(ks[6], (DEPTH, DEC_BATCH, DN_CONV - 1, 3 * DN_WIDTH), 1.0),
        'state_ffn_conv': normal(ks[7], (DEPTH, DEC_BATCH, FFN_CONV - 1, D_FF), 1.0),
        'norm_mix': gain(ks[8], (DEPTH, D_MODEL)),
        'w_in': normal(ks[9], (DEPTH, D_MODEL, N_IN), D_MODEL ** -0.5),
        'dn_conv_w': normal(ks[10], (DEPTH, DN_CONV, 3 * DN_WIDTH), DN_CONV ** -0.5),
        'dn_a_log': jnp.log(jax.random.uniform(ks[12], (DEPTH, DN_HEADS), f32, 1.0, 16.0)),
        'dn_dt_bias': dt + jnp.log(-jnp.expm1(-dt)),
        'dn_out_norm': gain(ks[13], (DEPTH, DN_HEAD_DIM)),
        'w_branch_dn': normal(ks[14], (DEPTH, DN_WIDTH, D_MODEL), DN_WIDTH ** -0.5),
        'w_branch_att': normal(ks[15], (DEPTH, ATT_GROUP_WIDTH, D_MODEL), ATT_GROUP_WIDTH ** -0.5),
        'w_out': normal(ks[16], (DEPTH, D_MODEL, D_MODEL), D_MODEL ** -0.5),
        'norm_ffn': gain(ks[17], (DEPTH, D_MODEL)),
        'w_ffn_gate': normal(ks[18], (DEPTH, D_MODEL, D_FF), D_MODEL ** -0.5),
        'w_ffn_up': normal(ks[19], (DEPTH, D_MODEL, D_FF), D_MODEL ** -0.5),
        'ffn_conv_w': normal(ks[20], (DEPTH, FFN_CONV, D_FF), FFN_CONV ** -0.5),
        'w_ffn_down': normal(ks[21], (DEPTH, D_FF, D_MODEL), D_FF ** -0.5),
        'norm_final': gain(ks[22], (D_MODEL,)),
    }


def reference(x_prompt, x_sample, cache_kv_w128, cache_kv_w512, cache_kv_w2048, state_dn,
              state_dn_conv, state_ffn_conv, norm_mix, w_in, dn_conv_w, dn_a_log, dn_dt_bias,
              dn_out_norm, w_branch_dn, w_branch_att, w_out, norm_ffn, w_ffn_gate, w_ffn_up,
              ffn_conv_w, w_ffn_down, norm_final):
    b_p, t_p, _ = x_prompt.shape
    t_s = x_sample.shape[1]
    pos_p = jnp.arange(t_p, dtype=jnp.int32)
    pos_s = PAST_LEN + jnp.arange(t_s, dtype=jnp.int32)
    h_p, h_s = x_prompt, x_sample
    prompt_states, sample_states = [], []
    for l in range(DEPTH):
        weights = (norm_mix[l], w_in[l], dn_conv_w[l], dn_a_log[l], dn_dt_bias[l], dn_out_norm[l],
                   w_branch_dn[l], w_branch_att[l], w_out[l], norm_ffn[l], w_ffn_gate[l],
                   w_ffn_up[l], ffn_conv_w[l], w_ffn_down[l])
        zero_dn_conv = jnp.zeros((b_p, DN_CONV - 1, 3 * DN_WIDTH), x_prompt.dtype)
        zero_dn = jnp.zeros((b_p, DN_HEADS, DN_HEAD_DIM, DN_HEAD_DIM), jnp.float32)
        zero_ffn_conv = jnp.zeros((b_p, FFN_CONV - 1, D_FF), x_prompt.dtype)
        h_p, st_p = decoder_layer(h_p, pos_p, zero_dn_conv, zero_dn, zero_ffn_conv, None, *weights)
        h_s, st_s = decoder_layer(h_s, pos_s, state_dn_conv[l], state_dn[l], state_ffn_conv[l],
                                  (cache_kv_w128[l], cache_kv_w512[l], cache_kv_w2048[l]), *weights)
        prompt_states.append(st_p)
        sample_states.append(st_s)
    y_prompt = rms_norm(h_p, norm_final)
    y_sample = rms_norm(h_s, norm_final)
    p_kv128, p_kv512, p_kv2048, p_dn, p_dn_conv, p_ffn_conv = stack_layers(prompt_states)
    s_kv128, s_kv512, s_kv2048, s_dn, s_dn_conv, s_ffn_conv = stack_layers(sample_states)
    return (y_prompt, y_sample, p_kv128, p_kv512, p_kv2048, p_dn, p_dn_conv, p_ffn_conv,
            s_kv128, s_kv512, s_kv2048, s_dn, s_dn_conv, s_ffn_conv)
```

```python
import functools
import math

import jax
import jax.numpy as jnp
from jax import lax
from jax.experimental import pallas as pl
from jax.experimental.pallas import tpu as pltpu

F32 = jnp.float32
BF16 = jnp.bfloat16

EPS = 1e-6
ROPE_THETA = 10000.0
HEAD_DIM = 128
DN_CHUNK = 64
DN_CONV = 4
FFN_CONV = 3
ATT_GROUPS = ((128, 1), (512, 4), (2048, 16))
ATT_BLOCK = 128
PAST_LEN = 8192
SUBLANES = 8
LANES = 128
VMEM_LIMIT = 56 * 1024 * 1024
HIGHEST = lax.Precision.HIGHEST


def _pick_tile(n, target, mult):
    best = None
    for t in range(mult, min(n, target) + 1, mult):
        if n % t == 0:
            best = t
    assert best is not None, (n, target, mult)
    return best


def _params(sem):
    return pltpu.CompilerParams(dimension_semantics=sem, vmem_limit_bytes=VMEM_LIMIT)


def _sigmoid(x):
    return 1.0 / (1.0 + jnp.exp(-x))


def _silu(x):
    return x * _sigmoid(x)


def _rmsnorm_body(x_ref, g_ref, o_ref):
    x = x_ref[...]
    ms = jnp.mean(x * x, axis=-1, keepdims=True)
    o_ref[...] = ((x * lax.rsqrt(ms + EPS)) * g_ref[...]).astype(o_ref.dtype)


def _rmsnorm(x, gain, out_dtype, *, row0=0, rows=None, tile=512):
    d = x.shape[1]
    rows = x.shape[0] if rows is None else rows
    tr = _pick_tile(math.gcd(rows, row0) if row0 else rows, tile, 16)
    off = row0 // tr
    return pl.pallas_call(
        _rmsnorm_body,
        grid=(rows // tr,),
        in_specs=[pl.BlockSpec((tr, d), lambda i: (i + off, 0)),
                  pl.BlockSpec((1, d), lambda i: (0, 0))],
        out_specs=pl.BlockSpec((tr, d), lambda i: (i, 0)),
        out_shape=jax.ShapeDtypeStruct((rows, d), out_dtype),
        compiler_params=_params(("parallel",)),
        name="rmsnorm",
    )(x, gain.reshape(1, d))


def _mm_body(epilogue, n_extra, a_ref, b_ref, *refs):
    extra = refs[:n_extra]
    o_ref = refs[n_extra]
    acc = jnp.dot(a_ref[...], b_ref[...], preferred_element_type=F32)
    o_ref[...] = epilogue(acc, *[e[...] for e in extra]).astype(o_ref.dtype)


def _matmul(a, b, *, out_dtype, tm, tn, epilogue=None, extras=(), name="matmul"):
    m, k = a.shape
    n = b.shape[1]
    if epilogue is None:
        epilogue = lambda acc: acc
    in_specs = [pl.BlockSpec((tm, k), lambda i, j: (i, 0)),
                pl.BlockSpec((k, tn), lambda i, j: (0, j))]
    args = [a, b]
    for arr, col0 in extras:
        assert col0 % tn == 0
        in_specs.append(pl.BlockSpec((tm, tn), lambda i, j, c=col0 // tn: (i, j + c)))
        args.append(arr)
    return pl.pallas_call(
        functools.partial(_mm_body, epilogue, len(extras)),
        grid=(m // tm, pl.cdiv(n, tn)),
        in_specs=in_specs,
        out_specs=pl.BlockSpec((tm, tn), lambda i, j: (i, j)),
        out_shape=jax.ShapeDtypeStruct((m, n), out_dtype),
        compiler_params=_params(("parallel", "parallel")),
        name=name,
    )(*args)


def _conv_body(k_width, zero_first, epilogue, n_extra, x_ref, prev_ref, w_ref, *refs):
    extra = refs[:n_extra]
    o_ref = refs[n_extra]
    buf = refs[n_extra + 1]
    tr = x_ref.shape[0]
    prev = prev_ref[...]
    if zero_first:
        prev = jnp.where(pl.program_id(0) == 0, jnp.zeros_like(prev), prev)
    buf[0:SUBLANES, :] = prev
    buf[SUBLANES:SUBLANES + tr, :] = x_ref[...]
    w = w_ref[...]
    y = x_ref[...] * w[k_width - 1:k_width, :]
    for k in range(k_width - 1):
        y = y + buf[pl.ds(SUBLANES - (k_width - 1) + k, tr), :] * w[k:k + 1, :]
    o_ref[...] = epilogue(y, *[e[...] for e in extra]).astype(o_ref.dtype)


def _conv(x, prev, w, *, k_width, rows, row0, tr, tc, col0, n_cols, zero_first,
          epilogue, extras=(), out_dtype, name):
    assert row0 % tr == 0 and rows % tr == 0 and col0 % tc == 0 and tr % SUBLANES == 0
    r_off, c_off = row0 // tr, col0 // tc
    sub = tr // SUBLANES
    if prev is None:
        prev_arr = x
        prev_spec = pl.BlockSpec(
            (SUBLANES, tc), lambda i, j: (jnp.maximum((i + r_off) * sub - 1, 0), j + c_off))
    else:
        prev_arr = prev
        prev_spec = pl.BlockSpec((SUBLANES, tc), lambda i, j: (i, j))
    in_specs = [pl.BlockSpec((tr, tc), lambda i, j: (i + r_off, j + c_off)),
                prev_spec,
                pl.BlockSpec((k_width, tc), lambda i, j: (0, j))]
    args = [x, prev_arr, w]
    for arr, ecol0 in extras:
        assert ecol0 % tc == 0
        in_specs.append(pl.BlockSpec((tr, tc), lambda i, j, c=ecol0 // tc: (i + r_off, j + c)))
        args.append(arr)
    return pl.pallas_call(
        functools.partial(_conv_body, k_width, zero_first, epilogue, len(extras)),
        grid=(rows // tr, pl.cdiv(n_cols, tc)),
        in_specs=in_specs,
        out_specs=pl.BlockSpec((tr, tc), lambda i, j: (i, j)),
        out_shape=jax.ShapeDtypeStruct((rows, n_cols), out_dtype),
        scratch_shapes=[pltpu.VMEM((tr + SUBLANES, tc), F32)],
        compiler_params=_params(("parallel", "parallel")),
        name=name,
    )(*args)


def _dn_act(dn_width, y):
    sec = pl.program_id(1)
    y = _silu(y)
    scale = jnp.where(sec == 0, HEAD_DIM ** -0.5, 1.0).astype(F32)
    outs = []
    for h in range(dn_width // HEAD_DIM):
        yh = y[:, h * HEAD_DIM:(h + 1) * HEAD_DIM]
        ss = jnp.sum(yh * yh, axis=-1, keepdims=True)
        yn = yh * lax.rsqrt(ss + EPS) * scale
        outs.append(jnp.where(sec < 2, yn, yh))
    return jnp.concatenate(outs, axis=-1)


def _ffn_act(y, up):
    return _silu(y) * up


def _gate_body(chunk, n_heads, ba_ref, alog_ref, dtb_ref, beta_ref, g_ref):
    x = ba_ref[...]
    tr = x.shape[0]
    beta_ref[...] = _sigmoid(x)
    z = x + dtb_ref[...]
    softplus = jnp.maximum(z, 0.0) + jnp.log1p(jnp.exp(-jnp.abs(z)))
    g = -jnp.exp(alog_ref[...]) * softplus
    i = lax.broadcasted_iota(jnp.int32, (tr, tr), 0)
    j = lax.broadcasted_iota(jnp.int32, (tr, tr), 1)
    tri = jnp.where((i // chunk == j // chunk) & (j <= i), 1.0, 0.0).astype(F32)
    g_ref[...] = jnp.dot(tri, g, preferred_element_type=F32, precision=HIGHEST)


def _gates(ba, a_log, dt_bias, *, rows, row0, chunk, tr):
    n_heads = a_log.shape[0]
    pad = lambda v: jnp.zeros((1, LANES), F32).at[0, n_heads:2 * n_heads].set(v.astype(F32))
    off = row0 // tr
    assert row0 % tr == 0 and rows % tr == 0 and tr % chunk == 0
    return pl.pallas_call(
        functools.partial(_gate_body, chunk, n_heads),
        grid=(rows // tr,),
        in_specs=[pl.BlockSpec((tr, LANES), lambda i: (i + off, 0)),
                  pl.BlockSpec((1, LANES), lambda i: (0, 0)),
                  pl.BlockSpec((1, LANES), lambda i: (0, 0))],
        out_specs=[pl.BlockSpec((tr, LANES), lambda i: (i, 0)),
                   pl.BlockSpec((tr, LANES), lambda i: (i, 0))],
        out_shape=[jax.ShapeDtypeStruct((rows, LANES), F32)] * 2,
        compiler_params=_params(("parallel",)),
        name="dn_gates",
    )(ba, pad(a_log), pad(dt_bias))


def _hdot(a, b):
    return jnp.dot(a, b, preferred_element_type=F32, precision=HIGHEST)


def _unit_lower_inverse(a_mat, ii, jj, chunk):
    n = a_mat.shape[0]
    base = min(chunk, 16)
    eye = jnp.where(ii == jj, 1.0, 0.0).astype(F32)
    d = jnp.where(ii // base == jj // base, a_mat, 0.0)
    p = eye - d
    dp = d
    for _ in range(int(math.log2(base)) - 1):
        dp = _hdot(dp, dp)
        p = p + _hdot(p, dp)
    size = base
    while size < chunk:
        e = jnp.where((ii // (2 * size) == jj // (2 * size)) & (ii // size != jj // size), a_mat, 0.0)
        p = p - _hdot(_hdot(p, e), p)
        size *= 2
    return p


def _dn_local_body(chunk, n_heads, hb, q_ref, k_ref, v_ref, beta_ref, g_ref,
                   u_ref, w_ref, qd_ref, kdt_ref, aqk_ref, gl_ref):
    n = q_ref.shape[0]
    hg = pl.program_id(0)
    ii = lax.broadcasted_iota(jnp.int32, (n, n), 0)
    jj = lax.broadcasted_iota(jnp.int32, (n, n), 1)
    same = ii // chunk == jj // chunk
    causal = same & (ii >= jj)
    strict = same & (ii > jj)
    last = jj == (ii // chunk) * chunk + (chunk - 1)
    beta_all = beta_ref[...]
    g_all = g_ref[...]
    lane = lax.broadcasted_iota(jnp.int32, (n, LANES), 1)
    for hh in range(hb):
        h = hg * hb + hh
        sl = slice(hh * HEAD_DIM, (hh + 1) * HEAD_DIM)
        q, k, v = q_ref[:, sl], k_ref[:, sl], v_ref[:, sl]
        beta = jnp.sum(jnp.where(lane == h, beta_all, 0.0), axis=-1, keepdims=True)
        gcum = jnp.sum(jnp.where(lane == h + n_heads, g_all, 0.0), axis=-1, keepdims=True)
        g_rows = jnp.broadcast_to(gcum, (n, n))
        g_cols = g_rows.T
        g_last = jnp.sum(jnp.where(last, g_cols, 0.0), axis=-1, keepdims=True)
        gamma = jnp.exp(jnp.where(causal, g_rows - g_cols, -jnp.inf))
        kb = k.astype(BF16)
        kk = lax.dot_general(kb, kb, (((1,), (1,)), ((), ())), preferred_element_type=F32)
        qk = lax.dot_general(q.astype(BF16), kb, (((1,), (1,)), ((), ())), preferred_element_type=F32)
        a_mat = jnp.where(strict, beta * kk * gamma, 0.0)
        eg = jnp.exp(gcum)
        rhs = jnp.concatenate([v * beta, k * (beta * eg)], axis=-1)
        sol = _hdot(_unit_lower_inverse(a_mat, ii, jj, chunk), rhs)
        u_ref[hh] = sol[:, :HEAD_DIM]
        w_ref[hh] = sol[:, HEAD_DIM:]
        qd_ref[hh] = q * eg
        kdt_ref[hh] = (k * jnp.exp(g_last - gcum)).T
        aqk_ref[hh] = qk * gamma
        gl_ref[hh] = jnp.broadcast_to(jnp.exp(g_last), (n, LANES))


def _dn_local(qkv, beta, gcum, *, rows, row0, chunk, n_heads, dn_width, hb, n):
    assert rows % n == 0 and row0 % n == 0 and n % chunk == 0
    off = row0 // n
    bw = hb * HEAD_DIM
    sec = dn_width // bw
    qkv_spec = lambda s: pl.BlockSpec((n, bw), lambda g, i, s=s: (i + off, g + s * sec))
    head_spec = lambda: pl.BlockSpec((hb, n, HEAD_DIM), lambda g, i: (g, i, 0))
    gate_spec = pl.BlockSpec((n, LANES), lambda g, i: (i, 0))
    shp = lambda *s: jax.ShapeDtypeStruct(s, F32)
    return pl.pallas_call(
        functools.partial(_dn_local_body, chunk, n_heads, hb),
        grid=(n_heads // hb, rows // n),
        in_specs=[qkv_spec(0), qkv_spec(1), qkv_spec(2), gate_spec, gate_spec],
        out_specs=[head_spec(), head_spec(), head_spec(),
                   pl.BlockSpec((hb, HEAD_DIM, n), lambda g, i: (g, 0, i)),
                   pl.BlockSpec((hb, n, n), lambda g, i: (g, i, 0)),
                   head_spec()],
        out_shape=[shp(n_heads, rows, HEAD_DIM), shp(n_heads, rows, HEAD_DIM),
                   shp(n_heads, rows, HEAD_DIM), shp(n_heads, HEAD_DIM, rows),
                   shp(n_heads, rows, n), shp(n_heads, rows, LANES)],
        compiler_params=_params(("parallel", "parallel")),
        name="dn_local",
    )(qkv, qkv, qkv, beta, gcum)


def _gated_head_norm(o, z, gain):
    ms = jnp.mean(o * o, axis=-1, keepdims=True)
    return ((o * lax.rsqrt(ms + EPS)) * gain) * _silu(z)


def _dn_scan_body(chunk, n_heads, u_ref, w_ref, qd_ref, kdt_ref, aqk_ref, gl_ref, z_ref, gain_ref,
                  o_ref, s_out_ref, s_ref):
    step = pl.program_id(0)
    n = u_ref.shape[1]

    @pl.when(step == 0)
    def _():
        s_ref[...] = jnp.zeros_like(s_ref)

    gain = gain_ref[...]
    for c in range(n // chunk):
        rows = slice(c * chunk, (c + 1) * chunk)
        for h in range(n_heads):
            s = s_ref[h]
            wq = jnp.concatenate([w_ref[h, rows, :], qd_ref[h, rows, :]], axis=0).astype(BF16)
            p = jnp.dot(wq, s.astype(BF16), preferred_element_type=F32)
            v_new = u_ref[h, rows, :] - p[:chunk]
            vb = v_new.astype(BF16)
            o = p[chunk:] + jnp.dot(aqk_ref[h, rows, rows].astype(BF16), vb, preferred_element_type=F32)
            s_ref[h] = s * gl_ref[h, c * chunk:c * chunk + 1, :] + jnp.dot(
                kdt_ref[h, :, rows].astype(BF16), vb, preferred_element_type=F32)
            cols = slice(h * HEAD_DIM, (h + 1) * HEAD_DIM)
            o_ref[rows, cols] = _gated_head_norm(o, z_ref[rows, cols], gain).astype(o_ref.dtype)

    @pl.when(step == pl.num_programs(0) - 1)
    def _():
        s_out_ref[...] = s_ref[...]


def _dn_scan(u, w, qd, kdt, aqk, gl, zsrc, z_col0, gain, *, chunk, n):
    n_heads, rows, _ = u.shape
    dn_width = n_heads * HEAD_DIM
    assert z_col0 % dn_width == 0
    head_spec = lambda: pl.BlockSpec((n_heads, n, HEAD_DIM), lambda i: (0, i, 0))
    return pl.pallas_call(
        functools.partial(_dn_scan_body, chunk, n_heads),
        grid=(rows // n,),
        in_specs=[head_spec(), head_spec(), head_spec(),
                  pl.BlockSpec((n_heads, HEAD_DIM, n), lambda i: (0, 0, i)),
                  pl.BlockSpec((n_heads, n, n), lambda i: (0, i, 0)),
                  head_spec(),
                  pl.BlockSpec((n, dn_width), lambda i: (i, z_col0 // dn_width)),
                  pl.BlockSpec((1, HEAD_DIM), lambda i: (0, 0))],
        out_specs=[pl.BlockSpec((n, dn_width), lambda i: (i, 0)),
                   pl.BlockSpec((n_heads, HEAD_DIM, HEAD_DIM), lambda i: (0, 0, 0))],
        out_shape=[jax.ShapeDtypeStruct((rows, dn_width), BF16),
                   jax.ShapeDtypeStruct((n_heads, HEAD_DIM, HEAD_DIM), F32)],
        scratch_shapes=[pltpu.VMEM((n_heads, HEAD_DIM, HEAD_DIM), F32)],
        compiler_params=_params(("arbitrary",)),
        name="dn_scan",
    )(u, w, qd, kdt, aqk, gl, zsrc, gain.reshape(1, HEAD_DIM))


def _dn_sample_body(t_len, u_ref, w_ref, qd_ref, kdt_ref, aqk_ref, gl_ref, z_ref, gain_ref, s_ref,
                    o_ref, s_out_ref):
    n = u_ref.shape[1]
    nb = n // t_len
    s = s_ref[0, :, 0]
    w3 = w_ref[0].reshape(nb, t_len, HEAD_DIM)
    q3 = qd_ref[0].reshape(nb, t_len, HEAD_DIM)
    wq = jnp.concatenate([w3, q3], axis=1).astype(BF16)
    p = jnp.einsum('bck,bkd->bcd', wq, s.astype(BF16), preferred_element_type=F32)
    ws = p[:, :t_len].reshape(n, HEAD_DIM)
    qs = p[:, t_len:].reshape(n, HEAD_DIM)
    v_new = u_ref[0] - ws
    vb = v_new.astype(BF16)
    o = qs + jnp.dot(aqk_ref[0].astype(BF16), vb, preferred_element_type=F32)
    kdt = kdt_ref[0]
    ri = lax.broadcasted_iota(jnp.int32, (nb * HEAD_DIM, n), 0)
    ci = lax.broadcasted_iota(jnp.int32, (nb * HEAD_DIM, n), 1)
    zt = jnp.where(ri // HEAD_DIM == ci // t_len, jnp.tile(kdt, (nb, 1)), 0.0).astype(BF16)
    upd = jnp.dot(zt, vb, preferred_element_type=F32).reshape(nb, HEAD_DIM, HEAD_DIM)
    gl = gl_ref[0].reshape(nb, t_len, LANES)[:, 0:1, :]
    s_out_ref[0, :, 0] = s * gl + upd
    o_ref[...] = _gated_head_norm(o, z_ref[...], gain_ref[...]).astype(o_ref.dtype)


def _dn_sample(u, w, qd, kdt, aqk, gl, zsrc, z_row0, z_col0, gain, state, *, t_len):
    n_heads, n, _ = u.shape
    nb = n // t_len
    assert z_row0 % n == 0 and z_col0 % HEAD_DIM == 0
    head_spec = lambda: pl.BlockSpec((1, n, HEAD_DIM), lambda h: (h, 0, 0))
    state_spec = pl.BlockSpec((1, nb, 1, HEAD_DIM, HEAD_DIM), lambda h: (0, 0, h, 0, 0))
    return pl.pallas_call(
        functools.partial(_dn_sample_body, t_len),
        grid=(n_heads,),
        in_specs=[head_spec(), head_spec(), head_spec(),
                  pl.BlockSpec((1, HEAD_DIM, n), lambda h: (h, 0, 0)),
                  pl.BlockSpec((1, n, n), lambda h: (h, 0, 0)),
                  head_spec(),
                  pl.BlockSpec((n, HEAD_DIM), lambda h: (z_row0 // n, z_col0 // HEAD_DIM + h)),
                  pl.BlockSpec((1, HEAD_DIM), lambda h: (0, 0)),
                  state_spec],
        out_specs=[pl.BlockSpec((n, HEAD_DIM), lambda h: (0, h)), state_spec],
        out_shape=[jax.ShapeDtypeStruct((n, n_heads * HEAD_DIM), BF16),
                   jax.ShapeDtypeStruct(state.shape, F32)],
        compiler_params=_params(("parallel",)),
        name="dn_sample",
    )(u, w, qd, kdt, aqk, gl, zsrc, gain.reshape(1, HEAD_DIM), state)


def _rope_body(n_heads, q_ref, k_ref, v_ref, cos_ref, sin_ref, qo_ref, kvo_ref):
    cos, sin = cos_ref[...], sin_ref[...]
    width = n_heads * HEAD_DIM
    for h in range(n_heads):
        sl = slice(h * HEAD_DIM, (h + 1) * HEAD_DIM)
        q = q_ref[:, sl]
        k = k_ref[:, sl]
        qo_ref[0, :, sl] = (q * cos + pltpu.roll(q, HEAD_DIM // 2, 1) * sin).astype(qo_ref.dtype)
        kvo_ref[0, :, sl] = k * cos + pltpu.roll(k, HEAD_DIM // 2, 1) * sin
    kvo_ref[0, :, width:] = v_ref[...]


def _rope(att_src, col0, cos, sin, *, rows, row0, n_groups, n_heads, tr):
    width = n_heads * HEAD_DIM
    assert row0 % tr == 0 and rows % tr == 0 and col0 % width == 0
    r_off, c_off = row0 // tr, col0 // width
    src = lambda s: pl.BlockSpec((tr, width), lambda g, i, s=s: (i + r_off, c_off + 3 * g + s))
    tab = pl.BlockSpec((tr, HEAD_DIM), lambda g, i: (i, 0))
    return pl.pallas_call(
        functools.partial(_rope_body, n_heads),
        grid=(n_groups, rows // tr),
        in_specs=[src(0), src(1), src(2), tab, tab],
        out_specs=[pl.BlockSpec((1, tr, width), lambda g, i: (g, i, 0)),
                   pl.BlockSpec((1, tr, 2 * width), lambda g, i: (g, i, 0))],
        out_shape=[jax.ShapeDtypeStruct((n_groups, rows, width), BF16),
                   jax.ShapeDtypeStruct((n_groups, rows, 2 * width), F32)],
        compiler_params=_params(("parallel", "parallel")),
        name="rope",
    )(att_src, att_src, att_src, cos, sin)


def _rope_tables(pos):
    half = HEAD_DIM // 2
    inv_freq = ROPE_THETA ** (-jnp.arange(half, dtype=F32) / half)
    ang = pos.astype(F32)[:, None] * inv_freq[None, :]
    cos, sin = jnp.cos(ang), jnp.sin(ang)
    return jnp.concatenate([cos, cos], axis=-1), jnp.concatenate([-sin, sin], axis=-1)


def _attn_prompt_body(n_heads, q_ref, kc_ref, kp_ref, vc_ref, vp_ref, o_ref, lse_ref):
    nb = ATT_BLOCK
    blk = pl.program_id(1)
    i = lax.broadcasted_iota(jnp.int32, (nb, 2 * nb), 0)
    j = lax.broadcasted_iota(jnp.int32, (nb, 2 * nb), 1)
    mask = (j >= i) & (j <= i + nb) & ((blk > 0) | (j >= nb))
    scale = HEAD_DIM ** -0.5
    for h in range(n_heads):
        sl = slice(h * HEAD_DIM, (h + 1) * HEAD_DIM)
        k = jnp.concatenate([kp_ref[:, sl], kc_ref[:, sl]], axis=0).astype(BF16)
        v = jnp.concatenate([vp_ref[:, sl], vc_ref[:, sl]], axis=0).astype(BF16)
        s = lax.dot_general(q_ref[:, sl], k, (((1,), (1,)), ((), ())), preferred_element_type=F32) * scale
        s = jnp.where(mask, s, -jnp.inf)
        m = jnp.max(s, axis=-1, keepdims=True)
        p = jnp.exp(s - m)
        l = jnp.sum(p, axis=-1, keepdims=True)
        o_ref[:, sl] = jnp.dot(p.astype(BF16), v, preferred_element_type=F32) / l
        lse_ref[:, sl] = jnp.broadcast_to(m + jnp.log(l), (nb, HEAD_DIM))


def _attn_prompt(q, kv, gi, dil, n_heads):
    n_groups, t_len, width = q.shape
    nb = ATT_BLOCK
    m_len = t_len // dil
    assert t_len % (dil * nb) == 0
    q2 = q.reshape(n_groups, m_len, dil * width)
    kv2 = kv.reshape(n_groups, m_len, dil * 2 * width)
    src = lambda f: pl.BlockSpec((None, nb, width), f)
    dst = lambda: pl.BlockSpec((nb, width), lambda r, n: (n, r))
    prev = lambda n: jnp.maximum(n - 1, 0)
    o, lse = pl.pallas_call(
        functools.partial(_attn_prompt_body, n_heads),
        grid=(dil, m_len // nb),
        in_specs=[src(lambda r, n: (gi, n, r)),
                  src(lambda r, n: (gi, n, 2 * r)), src(lambda r, n: (gi, prev(n), 2 * r)),
                  src(lambda r, n: (gi, n, 2 * r + 1)), src(lambda r, n: (gi, prev(n), 2 * r + 1))],
        out_specs=[dst(), dst()],
        out_shape=[jax.ShapeDtypeStruct((m_len, dil * width), F32)] * 2,
        compiler_params=_params(("parallel", "parallel")),
        name="attn_prompt",
    )(q2, kv2, kv2, kv2, kv2)
    return o.reshape(t_len, width), lse.reshape(t_len, width)


def _attn_sample_body(n_heads, dil, stride, q_ref, kvn_ref, cache_ref, o_ref, lse_ref):
    t_len = q_ref.shape[0]
    width = n_heads * HEAD_DIM
    n_buf = math.prod(cache_ref.shape[1:-1])

    def cache_cols(sl):
        if stride > 1:
            return cache_ref[0, :, :, sl].reshape(n_buf, HEAD_DIM).astype(BF16)
        return cache_ref[0, :, sl].astype(BF16)

    jq = lax.broadcasted_iota(jnp.int32, (t_len, n_buf), 0)
    cc = lax.broadcasted_iota(jnp.int32, (t_len, n_buf), 1)
    if stride > 1:
        idx = (cc // t_len) * stride + cc % t_len
    else:
        idx = cc
    mask_buf = (idx >= jq) & ((idx - jq) % dil == 0)
    jn = lax.broadcasted_iota(jnp.int32, (t_len, t_len), 0)
    cn = lax.broadcasted_iota(jnp.int32, (t_len, t_len), 1)
    mask_new = (cn <= jn) & ((jn - cn) % dil == 0)
    scale = HEAD_DIM ** -0.5
    nt = (((1,), (1,)), ((), ()))
    for h in range(n_heads):
        sl = slice(h * HEAD_DIM, (h + 1) * HEAD_DIM)
        slv = slice(width + h * HEAD_DIM, width + (h + 1) * HEAD_DIM)
        q = q_ref[:, sl]
        s_buf = lax.dot_general(q, cache_cols(sl), nt, preferred_element_type=F32) * scale
        s_new = lax.dot_general(q, kvn_ref[:, sl].astype(BF16), nt, preferred_element_type=F32) * scale
        s_buf = jnp.where(mask_buf, s_buf, -jnp.inf)
        s_new = jnp.where(mask_new, s_new, -jnp.inf)
        m = jnp.maximum(jnp.max(s_buf, axis=-1, keepdims=True), jnp.max(s_new, axis=-1, keepdims=True))
        p_buf = jnp.exp(s_buf - m)
        p_new = jnp.exp(s_new - m)
        l = jnp.sum(p_buf, axis=-1, keepdims=True) + jnp.sum(p_new, axis=-1, keepdims=True)
        o = (jnp.dot(p_buf.astype(BF16), cache_cols(slv), preferred_element_type=F32)
             + jnp.dot(p_new.astype(BF16), kvn_ref[:, slv].astype(BF16), preferred_element_type=F32))
        o_ref[:, sl] = o / l
        lse_ref[:, sl] = jnp.broadcast_to(m + jnp.log(l), (t_len, HEAD_DIM))


def _attn_sample(q, kvn, cache, win, dil, n_heads, t_len):
    bsz, buf_len, _ = cache.shape
    width = n_heads * HEAD_DIM
    assert buf_len == win and win == ATT_BLOCK * dil, "window buffer must be full"
    if dil >= 2 * t_len:
        stride = dil
        cache_v = cache.reshape(bsz, buf_len // dil, dil, 2 * width)
        cache_spec = pl.BlockSpec((1, buf_len // dil, t_len, 2 * width), lambda b: (b, 0, 0, 0))
    else:
        stride = 1
        cache_v = cache
        cache_spec = pl.BlockSpec((1, buf_len, 2 * width), lambda b: (b, 0, 0))
    return pl.pallas_call(
        functools.partial(_attn_sample_body, n_heads, dil, stride),
        grid=(bsz,),
        in_specs=[pl.BlockSpec((t_len, width), lambda b: (b, 0)),
                  pl.BlockSpec((t_len, 2 * width), lambda b: (b, 0)),
                  cache_spec],
        out_specs=[pl.BlockSpec((t_len, width), lambda b: (b, 0))] * 2,
        out_shape=[jax.ShapeDtypeStruct((bsz * t_len, width), F32)] * 2,
        compiler_params=_params(("parallel",)),
        name="attn_sample",
    )(q, kvn, cache_v)


def _merge_body(o0, o1, o2, l0, l1, l2, out_ref):
    a, b, c = l0[...], l1[...], l2[...]
    m = jnp.maximum(jnp.maximum(a, b), c)
    ea, eb, ec = jnp.exp(a - m), jnp.exp(b - m), jnp.exp(c - m)
    tot = ea + eb + ec
    out_ref[...] = ((ea * o0[...] + eb * o1[...] + ec * o2[...]) / tot).astype(out_ref.dtype)


def _merge(outs, lses, tr):
    rows, width = outs[0].shape
    spec = pl.BlockSpec((tr, width), lambda i: (i, 0))
    return pl.pallas_call(
        _merge_body,
        grid=(rows // tr,),
        in_specs=[spec] * 6,
        out_specs=spec,
        out_shape=jax.ShapeDtypeStruct((rows, width), BF16),
        compiler_params=_params(("parallel",)),
        name="merge_groups",
    )(*outs, *lses)


def _layer(x, t_p, bsz, t_s, past_len, caches, dn_state, dn_conv_state, ffn_conv_state,
           norm_mix, w_in, dn_conv_w, dn_a_log, dn_dt_bias, dn_out_norm, w_branch_dn,
           w_branch_att, w_out, norm_ffn, w_ffn_gate, w_ffn_up, ffn_conv_w, w_ffn_down):
    r_all, d_model = x.shape
    n_s = bsz * t_s
    n_heads_dn = dn_a_log.shape[0]
    dn_width = n_heads_dn * HEAD_DIM
    n_groups = len(ATT_GROUPS)
    att_width = w_branch_att.shape[0]
    n_heads_att = att_width // HEAD_DIM
    d_ff = w_ffn_gate.shape[1]
    tm = _pick_tile(r_all, 1100, 16)

    c_ba = 4 * dn_width
    c_att = c_ba + 2 * n_heads_dn
    w_a = w_in[:, :c_ba].astype(BF16)
    w_ba = jnp.pad(w_in[:, c_ba:c_att], ((0, 0), (0, LANES - 2 * n_heads_dn))).astype(BF16)
    w_rest = w_in[:, c_att:].astype(BF16)
    n1 = _rmsnorm(x, norm_mix, BF16)
    qkvz = _matmul(n1, w_a, out_dtype=F32, tm=tm, tn=1024, name="proj_dn")
    ba = _matmul(n1, w_ba, out_dtype=F32, tm=tm, tn=LANES, name="proj_ba")
    rest = _matmul(n1, w_rest, out_dtype=F32, tm=tm, tn=1024, name="proj_att")
    col_gdn = 3 * n_groups * att_width
    col_gatt = col_gdn + d_model

    dn_act = functools.partial(_dn_act, dn_width)
    conv_args = dict(k_width=DN_CONV, tc=dn_width, col0=0, n_cols=3 * dn_width,
                     epilogue=dn_act, out_dtype=F32)
    qkv_p = _conv(qkvz, None, dn_conv_w, rows=t_p, row0=0, tr=256, zero_first=True,
                  name="dn_conv_prompt", **conv_args)
    state_pad = jnp.pad(dn_conv_state, ((0, 0), (SUBLANES - (DN_CONV - 1), 0), (0, 0)))
    qkv_s = _conv(qkvz, state_pad.reshape(bsz * SUBLANES, -1), dn_conv_w, rows=n_s, row0=t_p,
                  tr=t_s, zero_first=False, name="dn_conv_sample", **conv_args)
    p_dn_conv = qkvz[t_p - (DN_CONV - 1):t_p, :3 * dn_width][None]
    s_dn_conv = qkvz[t_p:].reshape(bsz, t_s, -1)[:, t_s - (DN_CONV - 1):, :3 * dn_width]

    beta_p, g_p = _gates(ba, dn_a_log, dn_dt_bias, rows=t_p, row0=0, chunk=DN_CHUNK, tr=512)
    beta_s, g_s = _gates(ba, dn_a_log, dn_dt_bias, rows=n_s, row0=t_p, chunk=t_s, tr=n_s)

    loc_p = _dn_local(qkv_p, beta_p, g_p, rows=t_p, row0=0, chunk=DN_CHUNK,
                      n_heads=n_heads_dn, dn_width=dn_width, hb=4, n=128)
    o_dn_p, p_dn = _dn_scan(*loc_p, qkvz, 3 * dn_width, dn_out_norm, chunk=DN_CHUNK, n=128)
    loc_s = _dn_local(qkv_s, beta_s, g_s, rows=n_s, row0=0, chunk=t_s,
                      n_heads=n_heads_dn, dn_width=dn_width, hb=4, n=n_s)
    o_dn_s, s_dn = _dn_sample(*loc_s, qkvz, t_p, 3 * dn_width, dn_out_norm, dn_state, t_len=t_s)

    cos_p, sin_p = _rope_tables(jnp.arange(t_p, dtype=jnp.int32))
    cos_s, sin_s = _rope_tables(past_len + jnp.tile(jnp.arange(t_s, dtype=jnp.int32), bsz))
    rope_args = dict(n_groups=n_groups, n_heads=n_heads_att)
    q_p, kv_p = _rope(rest, 0, cos_p, sin_p, rows=t_p, row0=0, tr=256, **rope_args)
    q_s, kv_s = _rope(rest, 0, cos_s, sin_s, rows=n_s, row0=t_p, tr=n_s, **rope_args)
    outs_p, lses_p, outs_s, lses_s = [], [], [], []
    for gi, (win, dil) in enumerate(ATT_GROUPS):
        o, lse = _attn_prompt(q_p, kv_p, gi, dil, n_heads_att)
        outs_p.append(o)
        lses_p.append(lse)
        cache = caches[gi].reshape(bsz, caches[gi].shape[1], 2 * att_width)
        o, lse = _attn_sample(q_s[gi], kv_s[gi], cache, win, dil, n_heads_att, t_s)
        outs_s.append(o)
        lses_s.append(lse)
    o_att_p = _merge(outs_p, lses_p, 256)
    o_att_s = _merge(outs_s, lses_s, n_s)
    p_kv = [kv_p[gi][t_p - min(win, t_p):].reshape(1, min(win, t_p), 2, n_heads_att, HEAD_DIM)
            for gi, (win, _) in enumerate(ATT_GROUPS)]
    s_kv = [kv_s[gi].reshape(bsz, t_s, 2, n_heads_att, HEAD_DIM) for gi in range(n_groups)]

    o_dn = jnp.concatenate([o_dn_p, o_dn_s], axis=0)
    o_att = jnp.concatenate([o_att_p, o_att_s], axis=0)
    y_dn = _matmul(o_dn, w_branch_dn.astype(BF16), out_dtype=F32, tm=tm, tn=1024,
                   epilogue=lambda acc, g: _sigmoid(g) * acc, extras=[(rest, col_gdn)],
                   name="branch_dn")
    mix = _matmul(o_att, w_branch_att.astype(BF16), out_dtype=BF16, tm=tm, tn=1024,
                  epilogue=lambda acc, g, y: y + _sigmoid(g) * acc,
                  extras=[(rest, col_gatt), (y_dn, 0)], name="branch_att")
    x1 = _matmul(mix, w_out.astype(BF16), out_dtype=F32, tm=tm, tn=512,
                 epilogue=lambda acc, r: r + acc, extras=[(x, 0)], name="out_proj")

    n2 = _rmsnorm(x1, norm_ffn, BF16)
    gate = _matmul(n2, w_ffn_gate.astype(BF16), out_dtype=F32, tm=tm, tn=1024, name="ffn_gate")
    up = _matmul(n2, w_ffn_up.astype(BF16), out_dtype=F32, tm=tm, tn=1024, name="ffn_up")
    ffn_args = dict(k_width=FFN_CONV, tc=1024, col0=0, n_cols=d_ff, epilogue=_ffn_act,
                    extras=[(up, 0)], out_dtype=BF16)
    h_p = _conv(gate, None, ffn_conv_w, rows=t_p, row0=0, tr=256, zero_first=True,
                name="ffn_conv_prompt", **ffn_args)
    fstate_pad = jnp.pad(ffn_conv_state, ((0, 0), (SUBLANES - (FFN_CONV - 1), 0), (0, 0)))
    h_s = _conv(gate, fstate_pad.reshape(bsz * SUBLANES, -1), ffn_conv_w, rows=n_s, row0=t_p,
                tr=t_s, zero_first=False, name="ffn_conv_sample", **ffn_args)
    p_ffn_conv = gate[t_p - (FFN_CONV - 1):t_p][None]
    s_ffn_conv = gate[t_p:].reshape(bsz, t_s, d_ff)[:, t_s - (FFN_CONV - 1):]
    h = jnp.concatenate([h_p, h_s], axis=0)
    tm_down = _pick_tile(r_all, 600, 16)
    x2 = _matmul(h, w_ffn_down.astype(BF16), out_dtype=F32, tm=tm_down, tn=256,
                 epilogue=lambda acc, r: r + acc, extras=[(x1, 0)], name="ffn_down")
    states_p = (p_kv[0], p_kv[1], p_kv[2], p_dn[None], p_dn_conv, p_ffn_conv)
    states_s = (s_kv[0], s_kv[1], s_kv[2], s_dn[0], s_dn_conv, s_ffn_conv)
    return x2, states_p, states_s


def kernel(x_prompt, x_sample, cache_kv_w128, cache_kv_w512, cache_kv_w2048, state_dn, state_dn_conv, state_ffn_conv, norm_mix, w_in, dn_conv_w, dn_a_log, dn_dt_bias, dn_out_norm, w_branch_dn, w_branch_att, w_out, norm_ffn, w_ffn_gate, w_ffn_up, ffn_conv_w, w_ffn_down, norm_final):
    b_p, t_p, d_model = x_prompt.shape
    bsz, t_s, _ = x_sample.shape
    depth = w_in.shape[0]
    assert b_p == 1 and depth == 1, "one prompt sequence, one layer"
    x = jnp.concatenate([x_prompt.reshape(t_p, d_model), x_sample.reshape(bsz * t_s, d_model)], axis=0)
    l = 0
    x, st_p, st_s = _layer(
        x, t_p, bsz, t_s, PAST_LEN,
        (cache_kv_w128[l], cache_kv_w512[l], cache_kv_w2048[l]),
        state_dn[l:l + 1], state_dn_conv[l], state_ffn_conv[l],
        norm_mix[l], w_in[l], dn_conv_w[l], dn_a_log[l], dn_dt_bias[l], dn_out_norm[l],
        w_branch_dn[l], w_branch_att[l], w_out[l], norm_ffn[l], w_ffn_gate[l], w_ffn_up[l],
        ffn_conv_w[l], w_ffn_down[l])
    y_prompt = _rmsnorm(x, norm_final, F32, row0=0, rows=t_p).reshape(1, t_p, d_model)
    y_sample = _rmsnorm(x, norm_final, F32, row0=t_p, rows=bsz * t_s).reshape(bsz, t_s, d_model)
    return (y_prompt, y_sample) + tuple(s[None] for s in st_p) + tuple(s[None] for s in st_s)
```

```python
import functools
import math

import jax
import jax.numpy as jnp
from jax import lax
from jax.experimental import pallas as pl
from jax.experimental.pallas import tpu as pltpu

F32 = jnp.float32
BF16 = jnp.bfloat16

EPS = 1e-6
ROPE_THETA = 10000.0
HEAD_DIM = 128
DN_CHUNK = 64
DN_CONV = 4
FFN_CONV = 3
ATT_GROUPS = ((128, 1), (512, 4), (2048, 16))
ATT_BLOCK = 128
PAST_LEN = 8192
SUBLANES = 8
LANES = 128
VMEM_LIMIT = 56 * 1024 * 1024


def _pick_tile(n, target, mult):
    best = None
    for t in range(mult, min(n, target) + 1, mult):
        if n % t == 0:
            best = t
    assert best is not None, (n, target, mult)
    return best


def _params(sem):
    return pltpu.CompilerParams(dimension_semantics=sem, vmem_limit_bytes=VMEM_LIMIT)


def _sigmoid(x):
    return 1.0 / (1.0 + jnp.exp(-x))


def _silu(x):
    return x * _sigmoid(x)


def _into(into, args, in_specs):
    if into is None:
        return {}
    in_specs.append(pl.BlockSpec(memory_space=pl.ANY))
    args.append(into)
    return {len(args) - 1: 0}


def _rmsnorm_body(x_ref, g_ref, o_ref):
    x = x_ref[...]
    ms = jnp.mean(x * x, axis=-1, keepdims=True)
    o_ref[...] = ((x * lax.rsqrt(ms + EPS)) * g_ref[...]).astype(o_ref.dtype)


def _rmsnorm(x, gain, out_dtype, *, row0=0, rows=None, tile=512):
    d = x.shape[1]
    rows = x.shape[0] if rows is None else rows
    tr = _pick_tile(math.gcd(rows, row0) if row0 else rows, tile, 16)
    off = row0 // tr
    return pl.pallas_call(
        _rmsnorm_body,
        grid=(rows // tr,),
        in_specs=[pl.BlockSpec((tr, d), lambda i: (i + off, 0)),
                  pl.BlockSpec((1, d), lambda i: (0, 0))],
        out_specs=pl.BlockSpec((tr, d), lambda i: (i, 0)),
        out_shape=jax.ShapeDtypeStruct((rows, d), out_dtype),
        compiler_params=_params(("parallel",)),
        name="rmsnorm",
    )(x, gain.reshape(1, d))


def _mm_body(epilogue, n_extra, a_ref, b_ref, *refs):
    extra = refs[:n_extra]
    o_ref = refs[n_extra]
    acc = jnp.dot(a_ref[...], b_ref[...], preferred_element_type=F32)
    o_ref[...] = epilogue(acc, *[e[...] for e in extra]).astype(o_ref.dtype)


def _matmul(a, b, *, out_dtype, tm, tn, epilogue=None, extras=(), name="matmul"):
    m, k = a.shape
    n = b.shape[1]
    if epilogue is None:
        epilogue = lambda acc: acc
    in_specs = [pl.BlockSpec((tm, k), lambda i, j: (i, 0)),
                pl.BlockSpec((k, tn), lambda i, j: (0, j))]
    args = [a, b]
    for arr, col0 in extras:
        assert col0 % tn == 0
        in_specs.append(pl.BlockSpec((tm, tn), lambda i, j, c=col0 // tn: (i, j + c)))
        args.append(arr)
    return pl.pallas_call(
        functools.partial(_mm_body, epilogue, len(extras)),
        grid=(m // tm, pl.cdiv(n, tn)),
        in_specs=in_specs,
        out_specs=pl.BlockSpec((tm, tn), lambda i, j: (i, j)),
        out_shape=jax.ShapeDtypeStruct((m, n), out_dtype),
        compiler_params=_params(("parallel", "parallel")),
        name=name,
    )(*args)


def _mmw_body(epilogue, n_extra, shift, k_chunk, a_ref, b_ref, *refs):
    if shift:
        bn_ref, refs = refs[0], refs[1:]
    extra = refs[:n_extra]
    o_ref, w_scr = refs[n_extra], refs[n_extra + 1]

    @pl.when(pl.program_id(1) == 0)
    def _():
        for c in range(0, b_ref.shape[0], k_chunk):
            rows = slice(c, c + k_chunk)
            if shift:
                w = jnp.concatenate([b_ref[rows, shift:], bn_ref[rows, :shift]], axis=1)
            else:
                w = b_ref[rows, :]
            w_scr[rows, :] = w.astype(BF16)

    acc = jnp.dot(a_ref[...], w_scr[...], preferred_element_type=F32)
    o_ref[...] = epilogue(acc, *[e[...] for e in extra]).astype(o_ref.dtype)


def _matmul_w(a, b, *, col0=0, n=None, out_dtype, tm, tn, epilogue=None, extras=(), name="matmul_w"):
    m, k = a.shape
    n = b.shape[1] - col0 if n is None else n
    shift = col0 % LANES
    assert (col0 - shift) % tn == 0 and tn % LANES == 0
    jb = (col0 - shift) // tn
    if epilogue is None:
        epilogue = lambda acc: acc
    in_specs = [pl.BlockSpec((tm, k), lambda j, i: (i, 0)),
                pl.BlockSpec((k, tn), lambda j, i: (0, j + jb))]
    args = [a, b]
    if shift:
        per = tn // LANES
        in_specs.append(pl.BlockSpec((k, LANES), lambda j, i: (0, (j + jb + 1) * per)))
        args.append(b)
    for arr, ecol0 in extras:
        assert ecol0 % tn == 0
        in_specs.append(pl.BlockSpec((tm, tn), lambda j, i, c=ecol0 // tn: (i, j + c)))
        args.append(arr)
    return pl.pallas_call(
        functools.partial(_mmw_body, epilogue, len(extras), shift, min(k, 512)),
        grid=(pl.cdiv(n, tn), m // tm),
        in_specs=in_specs,
        out_specs=pl.BlockSpec((tm, tn), lambda j, i: (i, j)),
        out_shape=jax.ShapeDtypeStruct((m, n), out_dtype),
        scratch_shapes=[pltpu.VMEM((k, tn), BF16)],
        compiler_params=_params(("parallel", "arbitrary")),
        name=name,
    )(*args)


def _conv_body(k_width, zero_first, epilogue, n_extra, x_ref, prev_ref, w_ref, *refs):
    extra = refs[:n_extra]
    o_ref, buf = refs[-2], refs[-1]
    tr = x_ref.shape[0]
    prev = prev_ref[...]
    if zero_first:
        prev = jnp.where(pl.program_id(0) == 0, jnp.zeros_like(prev), prev)
    buf[0:SUBLANES, :] = prev
    buf[SUBLANES:SUBLANES + tr, :] = x_ref[...]
    w = w_ref[...]
    y = x_ref[...] * w[k_width - 1:k_width, :]
    for k in range(k_width - 1):
        y = y + buf[pl.ds(SUBLANES - (k_width - 1) + k, tr), :] * w[k:k + 1, :]
    o_ref[...] = epilogue(y, *[e[...] for e in extra]).astype(o_ref.dtype)


def _conv(x, prev, w, *, k_width, rows, row0, tr, tc, col0, n_cols, zero_first,
          epilogue, extras=(), out_dtype, name, out_rows=None, into=None):
    assert row0 % tr == 0 and rows % tr == 0 and col0 % tc == 0 and tr % SUBLANES == 0
    out_rows = rows if out_rows is None else out_rows
    o_off = 0 if into is None else row0 // tr
    r_off, c_off = row0 // tr, col0 // tc
    sub = tr // SUBLANES
    if prev is None:
        prev_arr = x
        prev_spec = pl.BlockSpec(
            (SUBLANES, tc), lambda i, j: (jnp.maximum((i + r_off) * sub - 1, 0), j + c_off))
    else:
        prev_arr = prev
        prev_spec = pl.BlockSpec((SUBLANES, tc), lambda i, j: (i, j))
    in_specs = [pl.BlockSpec((tr, tc), lambda i, j: (i + r_off, j + c_off)),
                prev_spec,
                pl.BlockSpec((k_width, tc), lambda i, j: (0, j))]
    args = [x, prev_arr, w]
    for arr, ecol0 in extras:
        assert ecol0 % tc == 0
        in_specs.append(pl.BlockSpec((tr, tc), lambda i, j, c=ecol0 // tc: (i + r_off, j + c)))
        args.append(arr)
    alias = _into(into, args, in_specs)
    if into is not None:
        assert into.shape[1] == n_cols and into.dtype == out_dtype
        out_rows = into.shape[0]
    return pl.pallas_call(
        functools.partial(_conv_body, k_width, zero_first, epilogue, len(extras)),
        grid=(rows // tr, pl.cdiv(n_cols, tc)),
        in_specs=in_specs,
        out_specs=pl.BlockSpec((tr, tc), lambda i, j: (i + o_off, j)),
        out_shape=jax.ShapeDtypeStruct((out_rows, n_cols), out_dtype),
        input_output_aliases=alias,
        scratch_shapes=[pltpu.VMEM((tr + SUBLANES, tc), F32)],
        compiler_params=_params(("parallel", "parallel")),
        name=name,
    )(*args)


def _dn_act(dn_width, y):
    sec = pl.program_id(1)
    y = _silu(y)
    scale = jnp.where(sec == 0, HEAD_DIM ** -0.5, 1.0).astype(F32)
    outs = []
    for h in range(dn_width // HEAD_DIM):
        yh = y[:, h * HEAD_DIM:(h + 1) * HEAD_DIM]
        ss = jnp.sum(yh * yh, axis=-1, keepdims=True)
        yn = yh * lax.rsqrt(ss + EPS) * scale
        outs.append(jnp.where(sec < 2, yn, yh))
    return jnp.concatenate(outs, axis=-1)


def _ffn_act(y, up):
    return _silu(y) * up


def _gate_body(chunk, ba_ref, alog_ref, dtb_ref, beta_ref, g_ref):
    x = ba_ref[...]
    tr = x.shape[0]
    beta_ref[...] = _sigmoid(x)
    z = x + dtb_ref[...]
    softplus = jnp.maximum(z, 0.0) + jnp.log1p(jnp.exp(-jnp.abs(z)))
    g = -jnp.exp(alog_ref[...]) * softplus
    i = lax.broadcasted_iota(jnp.int32, (tr, tr), 0)
    j = lax.broadcasted_iota(jnp.int32, (tr, tr), 1)
    tri = jnp.where((i // chunk == j // chunk) & (j <= i), 1.0, 0.0).astype(F32)
    g_ref[...] = jnp.dot(tri, g, preferred_element_type=F32, precision=lax.Precision.HIGHEST)


def _gates(ba, a_log, dt_bias, *, rows, row0, chunk, tr):
    n_heads = a_log.shape[0]
    pad = lambda v: jnp.zeros((1, LANES), F32).at[0, n_heads:2 * n_heads].set(v.astype(F32))
    off = row0 // tr
    assert row0 % tr == 0 and rows % tr == 0 and tr % chunk == 0
    return pl.pallas_call(
        functools.partial(_gate_body, chunk),
        grid=(rows // tr,),
        in_specs=[pl.BlockSpec((tr, LANES), lambda i: (i + off, 0)),
                  pl.BlockSpec((1, LANES), lambda i: (0, 0)),
                  pl.BlockSpec((1, LANES), lambda i: (0, 0))],
        out_specs=[pl.BlockSpec((tr, LANES), lambda i: (i, 0)),
                   pl.BlockSpec((tr, LANES), lambda i: (i, 0))],
        out_shape=[jax.ShapeDtypeStruct((rows, LANES), F32)] * 2,
        compiler_params=_params(("parallel",)),
        name="dn_gates",
    )(ba, pad(a_log), pad(dt_bias))


def _split3(x):
    hi = x.astype(BF16)
    return hi, (x - hi.astype(F32)).astype(BF16)


def _dot3(a, b):
    lhs = jnp.concatenate([a[0], a[0], a[1]], axis=1)
    rhs = jnp.concatenate([b[0], b[1], b[0]], axis=0)
    return jnp.dot(lhs, rhs, preferred_element_type=F32)


def _unit_lower_inverse(a_mats, ii, jj, chunk):
    base = min(chunk, 16)
    eye = jnp.where(ii == jj, 1.0, 0.0).astype(F32)
    in_base = ii // base == jj // base
    ds = [jnp.where(in_base, a, 0.0) for a in a_mats]
    ps = [eye - d for d in ds]
    dps = [_split3(d) for d in ds]
    for _ in range(int(math.log2(base)) - 1):
        dps = [_split3(_dot3(dp, dp)) for dp in dps]
        ps = [p + _dot3(_split3(p), dp) for p, dp in zip(ps, dps)]
    size = base
    while size < chunk:
        below = (ii // (2 * size) == jj // (2 * size)) & (ii // size != jj // size)
        es = [_split3(jnp.where(below, a, 0.0)) for a in a_mats]
        pss = [_split3(p) for p in ps]
        ts = [_split3(_dot3(p_s, e)) for p_s, e in zip(pss, es)]
        ps = [p - _dot3(t, p_s) for p, t, p_s in zip(ps, ts, pss)]
        size *= 2
    return ps


def _dn_local_body(chunk, n_heads, hb, q_ref, k_ref, v_ref, beta_ref, g_ref,
                   u_ref, w_ref, qd_ref, kdt_ref, aqk_ref, gl_ref):
    n = q_ref.shape[0]
    hg = pl.program_id(0)
    ii = lax.broadcasted_iota(jnp.int32, (n, n), 0)
    jj = lax.broadcasted_iota(jnp.int32, (n, n), 1)
    same = ii // chunk == jj // chunk
    causal = same & (ii >= jj)
    strict = same & (ii > jj)
    last = jj == (ii // chunk) * chunk + (chunk - 1)
    beta_all = beta_ref[...]
    g_all = g_ref[...]
    lane = lax.broadcasted_iota(jnp.int32, (n, LANES), 1)
    nt = (((1,), (1,)), ((), ()))
    a_mats, rhss = [], []
    for hh in range(hb):
        h = hg * hb + hh
        sl = slice(hh * HEAD_DIM, (hh + 1) * HEAD_DIM)
        q, k, v = q_ref[:, sl], k_ref[:, sl], v_ref[:, sl]
        beta = jnp.sum(jnp.where(lane == h, beta_all, 0.0), axis=-1, keepdims=True)
        gcum = jnp.sum(jnp.where(lane == h + n_heads, g_all, 0.0), axis=-1, keepdims=True)
        g_rows = jnp.broadcast_to(gcum, (n, n))
        g_cols = g_rows.T
        g_last = jnp.sum(jnp.where(last, g_cols, 0.0), axis=-1, keepdims=True)
        gamma = jnp.exp(jnp.where(causal, g_rows - g_cols, -jnp.inf))
        kb = k.astype(BF16)
        kk = lax.dot_general(kb, kb, nt, preferred_element_type=F32)
        qk = lax.dot_general(q.astype(BF16), kb, nt, preferred_element_type=F32)
        a_mats.append(jnp.where(strict, beta * kk * gamma, 0.0))
        eg = jnp.exp(gcum)
        rhss.append(_split3(jnp.concatenate([v * beta, k * (beta * eg)], axis=-1)))
        qd_ref[hh] = q * eg
        kdt_ref[hh] = (k * jnp.exp(g_last - gcum)).T
        aqk_ref[hh] = qk * gamma
        gl_ref[hh] = jnp.broadcast_to(jnp.exp(g_last), (n, LANES))
    invs = _unit_lower_inverse(a_mats, ii, jj, chunk)
    for hh in range(hb):
        sol = _dot3(_split3(invs[hh]), rhss[hh])
        u_ref[hh] = sol[:, :HEAD_DIM]
        w_ref[hh] = sol[:, HEAD_DIM:]


def _dn_local(qkv, beta, gcum, *, rows, row0, chunk, n_heads, dn_width, hb, n):
    assert rows % n == 0 and row0 % n == 0 and n % chunk == 0
    off = row0 // n
    bw = hb * HEAD_DIM
    sec = dn_width // bw
    qkv_spec = lambda s: pl.BlockSpec((n, bw), lambda g, i, s=s: (i + off, g + s * sec))
    head_spec = lambda: pl.BlockSpec((hb, n, HEAD_DIM), lambda g, i: (g, i, 0))
    gate_spec = pl.BlockSpec((n, LANES), lambda g, i: (i, 0))
    shp = lambda *s: jax.ShapeDtypeStruct(s, F32)
    return pl.pallas_call(
        functools.partial(_dn_local_body, chunk, n_heads, hb),
        grid=(n_heads // hb, rows // n),
        in_specs=[qkv_spec(0), qkv_spec(1), qkv_spec(2), gate_spec, gate_spec],
        out_specs=[head_spec(), head_spec(), head_spec(),
                   pl.BlockSpec((hb, HEAD_DIM, n), lambda g, i: (g, 0, i)),
                   pl.BlockSpec((hb, n, n), lambda g, i: (g, i, 0)),
                   head_spec()],
        out_shape=[shp(n_heads, rows, HEAD_DIM), shp(n_heads, rows, HEAD_DIM),
                   shp(n_heads, rows, HEAD_DIM), shp(n_heads, HEAD_DIM, rows),
                   shp(n_heads, rows, n), shp(n_heads, rows, LANES)],
        compiler_params=_params(("parallel", "parallel")),
        name="dn_local",
    )(qkv, qkv, qkv, beta, gcum)


def _gated_head_norm(o, z, gain):
    ms = jnp.mean(o * o, axis=-1, keepdims=True)
    return ((o * lax.rsqrt(ms + EPS)) * gain) * _silu(z)


def _dn_scan_body(chunk, n_heads, u_ref, w_ref, qd_ref, kdt_ref, aqk_ref, gl_ref, z_ref, gain_ref,
                  o_ref, s_out_ref, s_ref):
    step = pl.program_id(0)
    n = u_ref.shape[1]

    @pl.when(step == 0)
    def _():
        s_ref[...] = jnp.zeros_like(s_ref)

    gain = gain_ref[...]
    for c in range(n // chunk):
        rows = slice(c * chunk, (c + 1) * chunk)
        heads = range(n_heads)
        ps = [jnp.dot(jnp.concatenate([w_ref[h, rows, :], qd_ref[h, rows, :]], axis=0).astype(BF16),
                      s_ref[h].astype(BF16), preferred_element_type=F32) for h in heads]
        vbs = [(u_ref[h, rows, :] - ps[h][:chunk]).astype(BF16) for h in heads]
        for h in heads:
            s_ref[h] = s_ref[h] * gl_ref[h, c * chunk:c * chunk + 1, :] + jnp.dot(
                kdt_ref[h, :, rows].astype(BF16), vbs[h], preferred_element_type=F32)
        for h in heads:
            o = ps[h][chunk:] + jnp.dot(aqk_ref[h, rows, rows].astype(BF16), vbs[h],
                                        preferred_element_type=F32)
            cols = slice(h * HEAD_DIM, (h + 1) * HEAD_DIM)
            o_ref[rows, cols] = _gated_head_norm(o, z_ref[rows, cols], gain).astype(o_ref.dtype)

    @pl.when(step == pl.num_programs(0) - 1)
    def _():
        s_out_ref[...] = s_ref[...]


def _dn_scan(u, w, qd, kdt, aqk, gl, zsrc, z_col0, gain, *, chunk, n, out_rows):
    n_heads, rows, _ = u.shape
    dn_width = n_heads * HEAD_DIM
    assert z_col0 % dn_width == 0
    head_spec = lambda: pl.BlockSpec((n_heads, n, HEAD_DIM), lambda i: (0, i, 0))
    return pl.pallas_call(
        functools.partial(_dn_scan_body, chunk, n_heads),
        grid=(rows // n,),
        in_specs=[head_spec(), head_spec(), head_spec(),
                  pl.BlockSpec((n_heads, HEAD_DIM, n), lambda i: (0, 0, i)),
                  pl.BlockSpec((n_heads, n, n), lambda i: (0, i, 0)),
                  head_spec(),
                  pl.BlockSpec((n, dn_width), lambda i: (i, z_col0 // dn_width)),
                  pl.BlockSpec((1, HEAD_DIM), lambda i: (0, 0))],
        out_specs=[pl.BlockSpec((n, dn_width), lambda i: (i, 0)),
                   pl.BlockSpec((n_heads, HEAD_DIM, HEAD_DIM), lambda i: (0, 0, 0))],
        out_shape=[jax.ShapeDtypeStruct((out_rows, dn_width), BF16),
                   jax.ShapeDtypeStruct((n_heads, HEAD_DIM, HEAD_DIM), F32)],
        scratch_shapes=[pltpu.VMEM((n_heads, HEAD_DIM, HEAD_DIM), F32)],
        compiler_params=_params(("arbitrary",)),
        name="dn_scan",
    )(u, w, qd, kdt, aqk, gl, zsrc, gain.reshape(1, HEAD_DIM))


def _dn_sample_body(t_len, u_ref, w_ref, qd_ref, kdt_ref, aqk_ref, gl_ref, z_ref, gain_ref, s_ref,
                    into_ref, o_ref, s_out_ref):
    n = u_ref.shape[1]
    nb = n // t_len
    s = s_ref[0, :, 0]
    w3 = w_ref[0].reshape(nb, t_len, HEAD_DIM)
    q3 = qd_ref[0].reshape(nb, t_len, HEAD_DIM)
    wq = jnp.concatenate([w3, q3], axis=1).astype(BF16)
    p = jnp.einsum('bck,bkd->bcd', wq, s.astype(BF16), preferred_element_type=F32)
    ws = p[:, :t_len].reshape(n, HEAD_DIM)
    qs = p[:, t_len:].reshape(n, HEAD_DIM)
    v_new = u_ref[0] - ws
    vb = v_new.astype(BF16)
    o = qs + jnp.dot(aqk_ref[0].astype(BF16), vb, preferred_element_type=F32)
    kdt = kdt_ref[0]
    ri = lax.broadcasted_iota(jnp.int32, (nb * HEAD_DIM, n), 0)
    ci = lax.broadcasted_iota(jnp.int32, (nb * HEAD_DIM, n), 1)
    zt = jnp.where(ri // HEAD_DIM == ci // t_len, jnp.tile(kdt, (nb, 1)), 0.0).astype(BF16)
    upd = jnp.dot(zt, vb, preferred_element_type=F32).reshape(nb, HEAD_DIM, HEAD_DIM)
    gl = gl_ref[0].reshape(nb, t_len, LANES)[:, 0:1, :]
    s_out_ref[0, :, 0] = s * gl + upd
    o_ref[...] = _gated_head_norm(o, z_ref[...], gain_ref[...]).astype(o_ref.dtype)


def _dn_sample(u, w, qd, kdt, aqk, gl, zsrc, z_row0, z_col0, gain, state, into, *, t_len):
    n_heads, n, _ = u.shape
    nb = n // t_len
    assert z_row0 % n == 0 and z_col0 % HEAD_DIM == 0
    head_spec = lambda: pl.BlockSpec((1, n, HEAD_DIM), lambda h: (h, 0, 0))
    state_spec = pl.BlockSpec((1, nb, 1, HEAD_DIM, HEAD_DIM), lambda h: (0, 0, h, 0, 0))
    in_specs = [head_spec(), head_spec(), head_spec(),
                pl.BlockSpec((1, HEAD_DIM, n), lambda h: (h, 0, 0)),
                pl.BlockSpec((1, n, n), lambda h: (h, 0, 0)),
                head_spec(),
                pl.BlockSpec((n, HEAD_DIM), lambda h: (z_row0 // n, z_col0 // HEAD_DIM + h)),
                pl.BlockSpec((1, HEAD_DIM), lambda h: (0, 0)),
                state_spec]
    args = [u, w, qd, kdt, aqk, gl, zsrc, gain.reshape(1, HEAD_DIM), state]
    alias = _into(into, args, in_specs)
    return pl.pallas_call(
        functools.partial(_dn_sample_body, t_len),
        grid=(n_heads,),
        in_specs=in_specs,
        out_specs=[pl.BlockSpec((n, HEAD_DIM), lambda h: (z_row0 // n, h)), state_spec],
        out_shape=[jax.ShapeDtypeStruct(into.shape, into.dtype),
                   jax.ShapeDtypeStruct(state.shape, F32)],
        input_output_aliases=alias,
        compiler_params=_params(("parallel",)),
        name="dn_sample",
    )(*args)


def _rope_body(n_heads, dil, q_ref, k_ref, v_ref, cos_ref, sin_ref, qo_ref, kvo_ref, slab):
    width = n_heads * HEAD_DIM
    per = q_ref.shape[0] // dil
    cos, sin = cos_ref[...], sin_ref[...]

    def regroup(x, dst_ref, dst_sl):
        if dil == 1:
            dst_ref[0, :, dst_sl] = x.astype(dst_ref.dtype)
            return
        slab[...] = x

        def one_residue(r, carry):
            dst_ref[r, :, dst_sl] = slab[pl.ds(r, per, stride=dil), :].astype(dst_ref.dtype)
            return carry

        lax.fori_loop(0, dil, one_residue, 0)

    for h in range(n_heads):
        sl = slice(h * HEAD_DIM, (h + 1) * HEAD_DIM)
        q = q_ref[:, sl]
        k = k_ref[:, sl]
        regroup(q * cos + pltpu.roll(q, HEAD_DIM // 2, 1) * sin, qo_ref, sl)
        regroup(k * cos + pltpu.roll(k, HEAD_DIM // 2, 1) * sin, kvo_ref, sl)
        regroup(v_ref[:, sl], kvo_ref, slice(width + h * HEAD_DIM, width + (h + 1) * HEAD_DIM))


def _rope(att_src, col0, cos, sin, *, rows, row0, n_heads, tr, dil, q_dtype):
    width = n_heads * HEAD_DIM
    assert row0 % tr == 0 and rows % tr == 0 and col0 % width == 0 and tr % (16 * dil) == 0
    r_off, c_off = row0 // tr, col0 // width
    per = tr // dil
    src = lambda s: pl.BlockSpec((tr, width), lambda i, s=s: (i + r_off, c_off + s))
    tab = pl.BlockSpec((tr, HEAD_DIM), lambda i: (i, 0))
    return pl.pallas_call(
        functools.partial(_rope_body, n_heads, dil),
        grid=(rows // tr,),
        in_specs=[src(0), src(1), src(2), tab, tab],
        out_specs=[pl.BlockSpec((dil, per, width), lambda i: (0, i, 0)),
                   pl.BlockSpec((dil, per, 2 * width), lambda i: (0, i, 0))],
        out_shape=[jax.ShapeDtypeStruct((dil, rows // dil, width), q_dtype),
                   jax.ShapeDtypeStruct((dil, rows // dil, 2 * width), F32)],
        scratch_shapes=[pltpu.VMEM((tr, HEAD_DIM), F32)],
        compiler_params=_params(("parallel",)),
        name="rope",
    )(att_src, att_src, att_src, cos, sin)


def _rope_tables(pos):
    half = HEAD_DIM // 2
    inv_freq = ROPE_THETA ** (-jnp.arange(half, dtype=F32) / half)
    ang = pos.astype(F32)[:, None] * inv_freq[None, :]
    cos, sin = jnp.cos(ang), jnp.sin(ang)
    return jnp.concatenate([cos, cos], axis=-1), jnp.concatenate([-sin, sin], axis=-1)


def _attn_prompt_body(n_heads, dil, q_ref, kc_ref, kp_ref, vc_ref, vp_ref, o_ref, lse_ref):
    nb = ATT_BLOCK
    blk = pl.program_id(0)
    r = pl.program_id(1)
    rows = pl.ds(r, nb, stride=dil) if dil > 1 else slice(None)
    i = lax.broadcasted_iota(jnp.int32, (nb, 2 * nb), 0)
    j = lax.broadcasted_iota(jnp.int32, (nb, 2 * nb), 1)
    mask = (j >= i) & (j <= i + nb) & ((blk > 0) | (j >= nb))
    lane = lax.broadcasted_iota(jnp.int32, (nb, LANES), 1)
    scale = HEAD_DIM ** -0.5
    lse_all = jnp.zeros((nb, LANES), F32)
    for h in range(n_heads):
        sl = slice(h * HEAD_DIM, (h + 1) * HEAD_DIM)
        k = jnp.concatenate([kp_ref[:, sl], kc_ref[:, sl]], axis=0).astype(BF16)
        v = jnp.concatenate([vp_ref[:, sl], vc_ref[:, sl]], axis=0).astype(BF16)
        s = lax.dot_general(q_ref[:, sl], k, (((1,), (1,)), ((), ())), preferred_element_type=F32) * scale
        s = jnp.where(mask, s, -jnp.inf)
        m = jnp.max(s, axis=-1, keepdims=True)
        p = jnp.exp(s - m)
        l = jnp.sum(p, axis=-1, keepdims=True)
        o_ref[h, rows, :] = jnp.dot(p.astype(BF16), v, preferred_element_type=F32) / l
        lse_all = jnp.where(lane == h, m + jnp.log(l), lse_all)
    lse_ref[rows, :] = lse_all


def _attn_prompt(q, kv, dil, n_heads):
    _, m_len, width = q.shape
    nb = ATT_BLOCK
    t_len = m_len * dil
    assert m_len % nb == 0
    src = lambda f: pl.BlockSpec((None, nb, width), f)
    prev = lambda n: jnp.maximum(n - 1, 0)
    return pl.pallas_call(
        functools.partial(_attn_prompt_body, n_heads, dil),
        grid=(m_len // nb, dil),
        in_specs=[src(lambda n, r: (r, n, 0)),
                  src(lambda n, r: (r, n, 0)), src(lambda n, r: (r, prev(n), 0)),
                  src(lambda n, r: (r, n, 1)), src(lambda n, r: (r, prev(n), 1))],
        out_specs=[pl.BlockSpec((n_heads, nb * dil, HEAD_DIM), lambda n, r: (0, n, 0)),
                   pl.BlockSpec((nb * dil, LANES), lambda n, r: (n, 0))],
        out_shape=[jax.ShapeDtypeStruct((n_heads, t_len, HEAD_DIM), F32),
                   jax.ShapeDtypeStruct((t_len, LANES), F32)],
        compiler_params=_params(("parallel", "arbitrary")),
        name="attn_prompt",
    )(q, kv, kv, kv, kv)


def _attn_sample_body(n_heads, dil, stride, q_ref, kvn_ref, cache_ref, o_ref, lse_ref):
    t_len = q_ref.shape[1]
    width = n_heads * HEAD_DIM
    n_buf = math.prod(cache_ref.shape[1:-3])
    n_pair = n_buf * n_heads
    lg_h, lg_t = int(math.log2(n_heads)), int(math.log2(t_len))
    assert (1 << lg_h) == n_heads and (1 << lg_t) == t_len and dil & (dil - 1) == 0

    def cache_part(kv):
        if stride > 1:
            x = cache_ref[0, :, :, kv]
        else:
            x = cache_ref[0, :, kv]
        return x.reshape(n_pair, HEAD_DIM).astype(BF16)

    def heads_on_rows(ref, col0):
        return jnp.concatenate(
            [ref[0, :, col0 + h * HEAD_DIM:col0 + (h + 1) * HEAD_DIM] for h in range(n_heads)],
            axis=0).astype(BF16)

    q_all = heads_on_rows(q_ref, 0)
    k_new = heads_on_rows(kvn_ref, 0)
    v_new = heads_on_rows(kvn_ref, width)
    rq = lax.broadcasted_iota(jnp.int32, (n_heads * t_len, n_pair), 0)
    cc = lax.broadcasted_iota(jnp.int32, (n_heads * t_len, n_pair), 1)
    jq, hq = rq & (t_len - 1), rq >> lg_t
    row, hc = cc >> lg_h, cc & (n_heads - 1)
    if stride > 1:
        idx = (row >> lg_t) * stride + (row & (t_len - 1))
    else:
        idx = row
    mask_buf = (hq == hc) & (idx >= jq) & (((idx - jq) & (dil - 1)) == 0)
    rn = lax.broadcasted_iota(jnp.int32, (n_heads * t_len, n_heads * t_len), 0)
    cn = lax.broadcasted_iota(jnp.int32, (n_heads * t_len, n_heads * t_len), 1)
    jn, jc = rn & (t_len - 1), cn & (t_len - 1)
    mask_new = ((rn >> lg_t) == (cn >> lg_t)) & (jc <= jn) & (((jn - jc) & (dil - 1)) == 0)
    scale = HEAD_DIM ** -0.5
    nt = (((1,), (1,)), ((), ()))
    s_buf = lax.dot_general(q_all, cache_part(0), nt, preferred_element_type=F32) * scale
    s_new = lax.dot_general(q_all, k_new, nt, preferred_element_type=F32) * scale
    s_buf = jnp.where(mask_buf, s_buf, -jnp.inf)
    s_new = jnp.where(mask_new, s_new, -jnp.inf)
    m = jnp.maximum(jnp.max(s_buf, axis=-1, keepdims=True), jnp.max(s_new, axis=-1, keepdims=True))
    p_buf = jnp.exp(s_buf - m)
    p_new = jnp.exp(s_new - m)
    l = jnp.sum(p_buf, axis=-1, keepdims=True) + jnp.sum(p_new, axis=-1, keepdims=True)
    o = (jnp.dot(p_buf.astype(BF16), cache_part(1), preferred_element_type=F32)
         + jnp.dot(p_new.astype(BF16), v_new, preferred_element_type=F32)) / l
    lse = m + jnp.log(l)
    lane = lax.broadcasted_iota(jnp.int32, (t_len, LANES), 1)
    lse_all = jnp.zeros((t_len, LANES), F32)
    for h in range(n_heads):
        o_ref[h] = o[h * t_len:(h + 1) * t_len]
        lse_all = jnp.where(lane == h, lse[h * t_len:(h + 1) * t_len], lse_all)
    lse_ref[...] = lse_all


def _attn_sample(q, kvn, cache, win, dil, n_heads, t_len):
    bsz, buf_len = cache.shape[:2]
    width = n_heads * HEAD_DIM
    assert buf_len == win and win == ATT_BLOCK * dil, "window buffer must be full"
    if dil >= 2 * t_len:
        stride = dil
        cache_v = cache.reshape(bsz, buf_len // dil, dil, 2, n_heads, HEAD_DIM)
        cache_spec = pl.BlockSpec((1, buf_len // dil, t_len, 2, n_heads, HEAD_DIM),
                                  lambda b: (b, 0, 0, 0, 0, 0))
    else:
        stride = 1
        cache_v = cache
        cache_spec = pl.BlockSpec((1, buf_len, 2, n_heads, HEAD_DIM), lambda b: (b, 0, 0, 0, 0))
    return pl.pallas_call(
        functools.partial(_attn_sample_body, n_heads, dil, stride),
        grid=(bsz,),
        in_specs=[pl.BlockSpec((1, t_len, width), lambda b: (0, b, 0)),
                  pl.BlockSpec((1, t_len, 2 * width), lambda b: (0, b, 0)),
                  cache_spec],
        out_specs=[pl.BlockSpec((n_heads, t_len, HEAD_DIM), lambda b: (0, b, 0)),
                   pl.BlockSpec((t_len, LANES), lambda b: (b, 0))],
        out_shape=[jax.ShapeDtypeStruct((n_heads, bsz * t_len, HEAD_DIM), F32),
                   jax.ShapeDtypeStruct((bsz * t_len, LANES), F32)],
        compiler_params=_params(("parallel",)),
        name="attn_sample",
    )(q, kvn, cache_v)


def _merge_body(n_heads, o0, o1, o2, l0, l1, l2, *refs):
    out_ref = refs[-1]
    la, lb, lc = l0[...], l1[...], l2[...]
    m = jnp.maximum(jnp.maximum(la, lb), lc)
    ea, eb, ec = jnp.exp(la - m), jnp.exp(lb - m), jnp.exp(lc - m)
    tot = ea + eb + ec
    wa, wb, wc = ea / tot, eb / tot, ec / tot
    for h in range(n_heads):
        sl = slice(h * HEAD_DIM, (h + 1) * HEAD_DIM)
        col = slice(h, h + 1)
        out_ref[:, sl] = (wa[:, col] * o0[h] + wb[:, col] * o1[h]
                          + wc[:, col] * o2[h]).astype(out_ref.dtype)


def _merge(outs, lses, tr, *, out_rows=None, row0=0, into=None):
    n_heads, rows, _ = outs[0].shape
    width = n_heads * HEAD_DIM
    assert row0 % tr == 0 and rows % tr == 0
    off = row0 // tr
    spec = pl.BlockSpec((n_heads, tr, HEAD_DIM), lambda i: (0, i, 0))
    lspec = pl.BlockSpec((tr, LANES), lambda i: (i, 0))
    in_specs = [spec] * 3 + [lspec] * 3
    args = [*outs, *lses]
    alias = _into(into, args, in_specs)
    out_rows = into.shape[0] if into is not None else (rows if out_rows is None else out_rows)
    return pl.pallas_call(
        functools.partial(_merge_body, n_heads),
        grid=(rows // tr,),
        in_specs=in_specs,
        out_specs=pl.BlockSpec((tr, width), lambda i: (i + off, 0)),
        out_shape=jax.ShapeDtypeStruct((out_rows, width), BF16),
        input_output_aliases=alias,
        compiler_params=_params(("parallel",)),
        name="merge_groups",
    )(*args)


def _layer(x, t_p, bsz, t_s, past_len, caches, dn_state, dn_conv_state, ffn_conv_state,
           norm_mix, w_in, dn_conv_w, dn_a_log, dn_dt_bias, dn_out_norm, w_branch_dn,
           w_branch_att, w_out, norm_ffn, w_ffn_gate, w_ffn_up, ffn_conv_w, w_ffn_down):
    r_all, d_model = x.shape
    n_s = bsz * t_s
    n_heads_dn = dn_a_log.shape[0]
    dn_width = n_heads_dn * HEAD_DIM
    n_groups = len(ATT_GROUPS)
    att_width = w_branch_att.shape[0]
    n_heads_att = att_width // HEAD_DIM
    d_ff = w_ffn_gate.shape[1]
    tm = _pick_tile(r_all, 1100, 16)

    c_ba = 4 * dn_width
    c_att = c_ba + 2 * n_heads_dn
    n1 = _rmsnorm(x, norm_mix, BF16)
    qkvz = _matmul_w(n1, w_in, col0=0, n=c_ba, out_dtype=F32, tm=tm, tn=512, name="proj_dn")
    ba = _matmul_w(n1, w_in, col0=c_ba, n=LANES, out_dtype=F32, tm=tm, tn=LANES, name="proj_ba")
    rest = _matmul_w(n1, w_in, col0=c_att, out_dtype=F32, tm=tm, tn=512, name="proj_att")
    col_gdn = 3 * n_groups * att_width
    col_gatt = col_gdn + d_model

    dn_act = functools.partial(_dn_act, dn_width)
    conv_args = dict(k_width=DN_CONV, tc=dn_width, col0=0, n_cols=3 * dn_width,
                     epilogue=dn_act, out_dtype=F32)
    qkv_p = _conv(qkvz, None, dn_conv_w, rows=t_p, row0=0, tr=256, zero_first=True,
                  name="dn_conv_prompt", **conv_args)
    state_pad = jnp.pad(dn_conv_state, ((0, 0), (SUBLANES - (DN_CONV - 1), 0), (0, 0)))
    qkv_s = _conv(qkvz, state_pad.reshape(bsz * SUBLANES, -1), dn_conv_w, rows=n_s, row0=t_p,
                  tr=t_s, zero_first=False, name="dn_conv_sample", **conv_args)
    p_dn_conv = qkvz[t_p - (DN_CONV - 1):t_p, :3 * dn_width][None]
    s_dn_conv = qkvz[t_p:].reshape(bsz, t_s, -1)[:, t_s - (DN_CONV - 1):, :3 * dn_width]

    beta_p, g_p = _gates(ba, dn_a_log, dn_dt_bias, rows=t_p, row0=0, chunk=DN_CHUNK, tr=512)
    beta_s, g_s = _gates(ba, dn_a_log, dn_dt_bias, rows=n_s, row0=t_p, chunk=t_s, tr=n_s)

    loc_p = _dn_local(qkv_p, beta_p, g_p, rows=t_p, row0=0, chunk=DN_CHUNK,
                      n_heads=n_heads_dn, dn_width=dn_width, hb=8, n=128)
    o_dn, p_dn = _dn_scan(*loc_p, qkvz, 3 * dn_width, dn_out_norm, chunk=DN_CHUNK, n=128,
                          out_rows=r_all)
    loc_s = _dn_local(qkv_s, beta_s, g_s, rows=n_s, row0=0, chunk=t_s,
                      n_heads=n_heads_dn, dn_width=dn_width, hb=4, n=n_s)
    o_dn, s_dn = _dn_sample(*loc_s, qkvz, t_p, 3 * dn_width, dn_out_norm, dn_state, o_dn, t_len=t_s)

    cos_p, sin_p = _rope_tables(jnp.arange(t_p, dtype=jnp.int32))
    cos_s, sin_s = _rope_tables(past_len + jnp.tile(jnp.arange(t_s, dtype=jnp.int32), bsz))
    outs_p, lses_p, outs_s, lses_s, p_kv, s_kv = [], [], [], [], [], []
    for gi, (win, dil) in enumerate(ATT_GROUPS):
        col = 3 * gi * att_width
        q_p, kv_p = _rope(rest, col, cos_p, sin_p, rows=t_p, row0=0, n_heads=n_heads_att,
                          tr=512, dil=dil, q_dtype=BF16)
        q_s, kv_s = _rope(rest, col, cos_s, sin_s, rows=n_s, row0=t_p, n_heads=n_heads_att,
                          tr=n_s, dil=1, q_dtype=F32)
        o, lse = _attn_prompt(q_p, kv_p, dil, n_heads_att)
        outs_p.append(o)
        lses_p.append(lse)
        o, lse = _attn_sample(q_s, kv_s, caches[gi], win, dil, n_heads_att, t_s)
        outs_s.append(o)
        lses_s.append(lse)
        keep = min(win, t_p)
        tail = kv_p[:, (t_p - keep) // dil:].transpose(1, 0, 2)
        p_kv.append(tail.reshape(1, keep, 2, n_heads_att, HEAD_DIM))
        s_kv.append(kv_s.reshape(bsz, t_s, 2, n_heads_att, HEAD_DIM))
    o_att = _merge(outs_p, lses_p, 256, out_rows=r_all)
    o_att = _merge(outs_s, lses_s, n_s, row0=t_p, into=o_att)

    y_dn = _matmul_w(o_dn, w_branch_dn, out_dtype=F32, tm=tm, tn=512,
                     epilogue=lambda acc, g: _sigmoid(g) * acc, extras=[(rest, col_gdn)],
                     name="branch_dn")
    mix = _matmul_w(o_att, w_branch_att, out_dtype=BF16, tm=tm, tn=512,
                    epilogue=lambda acc, g, y: y + _sigmoid(g) * acc,
                    extras=[(rest, col_gatt), (y_dn, 0)], name="branch_att")
    x1 = _matmul_w(mix, w_out, out_dtype=F32, tm=tm, tn=512,
                   epilogue=lambda acc, r: r + acc, extras=[(x, 0)], name="out_proj")

    n2 = _rmsnorm(x1, norm_ffn, BF16)
    gate = _matmul_w(n2, w_ffn_gate, out_dtype=F32, tm=tm, tn=512, name="ffn_gate")
    up = _matmul_w(n2, w_ffn_up, out_dtype=F32, tm=tm, tn=512, name="ffn_up")
    ffn_args = dict(k_width=FFN_CONV, tc=1024, col0=0, n_cols=d_ff, epilogue=_ffn_act,
                    extras=[(up, 0)], out_dtype=BF16)
    h = _conv(gate, None, ffn_conv_w, rows=t_p, row0=0, tr=512, zero_first=True,
              name="ffn_conv_prompt", out_rows=r_all, **ffn_args)
    fstate_pad = jnp.pad(ffn_conv_state, ((0, 0), (SUBLANES - (FFN_CONV - 1), 0), (0, 0)))
    h = _conv(gate, fstate_pad.reshape(bsz * SUBLANES, -1), ffn_conv_w, rows=n_s, row0=t_p,
              tr=t_s, zero_first=False, name="ffn_conv_sample", into=h, **ffn_args)
    p_ffn_conv = gate[t_p - (FFN_CONV - 1):t_p][None]
    s_ffn_conv = gate[t_p:].reshape(bsz, t_s, d_ff)[:, t_s - (FFN_CONV - 1):]
    tm_down = _pick_tile(r_all, 600, 16)
    x2 = _matmul(h, w_ffn_down.astype(BF16), out_dtype=F32, tm=tm_down, tn=256,
                 epilogue=lambda acc, r: r + acc, extras=[(x1, 0)], name="ffn_down")
    states_p = (p_kv[0], p_kv[1], p_kv[2], p_dn[None], p_dn_conv, p_ffn_conv)
    states_s = (s_kv[0], s_kv[1], s_kv[2], s_dn[0], s_dn_conv, s_ffn_conv)
    return x2, states_p, states_s


def kernel(x_prompt, x_sample, cache_kv_w128, cache_kv_w512, cache_kv_w2048, state_dn, state_dn_conv, state_ffn_conv, norm_mix, w_in, dn_conv_w, dn_a_log, dn_dt_bias, dn_out_norm, w_branch_dn, w_branch_att, w_out, norm_ffn, w_ffn_gate, w_ffn_up, ffn_conv_w, w_ffn_down, norm_final):
    b_p, t_p, d_model = x_prompt.shape
    bsz, t_s, _ = x_sample.shape
    depth = w_in.shape[0]
    assert b_p == 1 and depth == 1, "one prompt sequence, one layer"
    x = jnp.concatenate([x_prompt.reshape(t_p, d_model), x_sample.reshape(bsz * t_s, d_model)], axis=0)
    l = 0
    x, st_p, st_s = _layer(
        x, t_p, bsz, t_s, PAST_LEN,
        (cache_kv_w128[l], cache_kv_w512[l], cache_kv_w2048[l]),
        state_dn[l:l + 1], state_dn_conv[l], state_ffn_conv[l],
        norm_mix[l], w_in[l], dn_conv_w[l], dn_a_log[l], dn_dt_bias[l], dn_out_norm[l],
        w_branch_dn[l], w_branch_att[l], w_out[l], norm_ffn[l], w_ffn_gate[l], w_ffn_up[l],
        ffn_conv_w[l], w_ffn_down[l])
    y_prompt = _rmsnorm(x, norm_final, F32, row0=0, rows=t_p).reshape(1, t_p, d_model)
    y_sample = _rmsnorm(x, norm_final, F32, row0=t_p, rows=bsz * t_s).reshape(bsz, t_s, d_model)
    return (y_prompt, y_sample) + tuple(s[None] for s in st_p) + tuple(s[None] for s in st_s)
```

```python
import functools
import math

import jax
import jax.numpy as jnp
from jax import lax
from jax.experimental import pallas as pl
from jax.experimental.pallas import tpu as pltpu

F32 = jnp.float32
BF16 = jnp.bfloat16

EPS = 1e-6
ROPE_THETA = 10000.0
HEAD_DIM = 128
DN_CHUNK = 64
DN_CONV = 4
FFN_CONV = 3
ATT_GROUPS = ((128, 1), (512, 4), (2048, 16))
ATT_BLOCK = 128
PAST_LEN = 8192
SUBLANES = 8
LANES = 128
VMEM_LIMIT = 60 * 1024 * 1024


def _pick_tile(n, target, mult):
    best = None
    for t in range(mult, min(n, target) + 1, mult):
        if n % t == 0:
            best = t
    assert best is not None, (n, target, mult)
    return best


def _params(sem):
    return pltpu.CompilerParams(dimension_semantics=sem, vmem_limit_bytes=VMEM_LIMIT)


def _sigmoid(x):
    return 1.0 / (1.0 + jnp.exp(-x))


def _silu(x):
    return x * _sigmoid(x)


def _into(into, args, in_specs):
    if into is None:
        return {}
    in_specs.append(pl.BlockSpec(memory_space=pl.ANY))
    args.append(into)
    return {len(args) - 1: 0}


def _rmsnorm_body(x_ref, g_ref, o_ref):
    x = x_ref[...]
    ms = jnp.mean(x * x, axis=-1, keepdims=True)
    o_ref[...] = ((x * lax.rsqrt(ms + EPS)) * g_ref[...]).astype(o_ref.dtype)


def _rmsnorm(x, gain, out_dtype, *, row0=0, rows=None, tile=512):
    d = x.shape[1]
    rows = x.shape[0] if rows is None else rows
    tr = _pick_tile(math.gcd(rows, row0) if row0 else rows, tile, 16)
    off = row0 // tr
    return pl.pallas_call(
        _rmsnorm_body,
        grid=(rows // tr,),
        in_specs=[pl.BlockSpec((tr, d), lambda i: (i + off, 0)),
                  pl.BlockSpec((1, d), lambda i: (0, 0))],
        out_specs=pl.BlockSpec((tr, d), lambda i: (i, 0)),
        out_shape=jax.ShapeDtypeStruct((rows, d), out_dtype),
        compiler_params=_params(("parallel",)),
        name="rmsnorm",
    )(x, gain.reshape(1, d))


def _mm_body(epilogue, n_extra, a_ref, b_ref, *refs):
    extra = refs[:n_extra]
    o_ref = refs[n_extra]
    acc = jnp.dot(a_ref[...], b_ref[...], preferred_element_type=F32)
    o_ref[...] = epilogue(acc, *[e[...] for e in extra]).astype(o_ref.dtype)


def _matmul(a, b, *, out_dtype, tm, tn, epilogue=None, extras=(), name="matmul"):
    m, k = a.shape
    n = b.shape[1]
    if epilogue is None:
        epilogue = lambda acc: acc
    in_specs = [pl.BlockSpec((tm, k), lambda i, j: (i, 0)),
                pl.BlockSpec((k, tn), lambda i, j: (0, j))]
    args = [a, b]
    for arr, col0 in extras:
        assert col0 % tn == 0
        in_specs.append(pl.BlockSpec((tm, tn), lambda i, j, c=col0 // tn: (i, j + c)))
        args.append(arr)
    return pl.pallas_call(
        functools.partial(_mm_body, epilogue, len(extras)),
        grid=(m // tm, pl.cdiv(n, tn)),
        in_specs=in_specs,
        out_specs=pl.BlockSpec((tm, tn), lambda i, j: (i, j)),
        out_shape=jax.ShapeDtypeStruct((m, n), out_dtype),
        compiler_params=_params(("parallel", "parallel")),
        name=name,
    )(*args)


def _mmw_body(epilogue, n_extra, shift, transposed, a_ref, b_ref, *refs):
    if shift:
        bn_ref, refs = refs[0], refs[1:]
    extra = refs[:n_extra]
    o_ref, w_scr = refs[n_extra], refs[n_extra + 1]
    slot = pl.program_id(2)

    @pl.when(pl.program_id(1) == 0)
    def _():
        n_rows = b_ref.shape[0]
        step = min(n_rows, 256)
        for c in range(0, n_rows, step):
            rows = slice(c, c + step)
            if not shift:
                w = b_ref[rows, :]
            elif not transposed:
                w = jnp.concatenate([b_ref[rows, shift:], bn_ref[rows, :shift]], axis=1)
            elif c + step < n_rows:
                w = b_ref[c + shift:c + step + shift, :]
            else:
                w = jnp.concatenate([b_ref[c + shift:, :], bn_ref[...]], axis=0)
            w_scr[slot, rows, :] = w.astype(BF16)

    dims = (((1,), (1,)), ((), ())) if transposed else (((1,), (0,)), ((), ()))
    acc = lax.dot_general(a_ref[...], w_scr[slot], dims, preferred_element_type=F32)
    o_ref[...] = epilogue(acc, *[e[...] for e in extra]).astype(o_ref.dtype)


def _matmul_w(a, b, *, col0=0, n=None, transposed=False, out_dtype, tm, tn, epilogue=None,
              extras=(), name="matmul_w"):
    m, k = a.shape
    n_total = b.shape[0] if transposed else b.shape[1]
    n = n_total - col0 if n is None else n
    shift = col0 % tn if transposed else col0 % LANES
    assert (col0 - shift) % tn == 0 and tn % LANES == 0
    if transposed and shift:
        assert tn % shift == 0 and shift % SUBLANES == 0
    jb = (col0 - shift) // tn
    n_tiles = pl.cdiv(n, tn)
    pz = 2 if n_tiles % 2 == 0 else 1
    if epilogue is None:
        epilogue = lambda acc: acc

    def tile(jp, i, jj):
        return jp * pz + jnp.where(i == 0, jj, pz - 1) + jb

    if transposed:
        b_spec = pl.BlockSpec((tn, k), lambda jp, i, jj: (tile(jp, i, jj), 0))
        bn_spec = pl.BlockSpec((shift or SUBLANES, k),
                               lambda jp, i, jj: ((tile(jp, i, jj) + 1) * (tn // (shift or tn)), 0))
        scr = pltpu.VMEM((pz, tn, k), BF16)
    else:
        b_spec = pl.BlockSpec((k, tn), lambda jp, i, jj: (0, tile(jp, i, jj)))
        bn_spec = pl.BlockSpec((k, LANES), lambda jp, i, jj: (0, (tile(jp, i, jj) + 1) * (tn // LANES)))
        scr = pltpu.VMEM((pz, k, tn), BF16)
    in_specs = [pl.BlockSpec((tm, k), lambda jp, i, jj: (i, 0)), b_spec]
    args = [a, b]
    if shift:
        in_specs.append(bn_spec)
        args.append(b)
    for arr, ecol0 in extras:
        assert ecol0 % tn == 0
        in_specs.append(pl.BlockSpec((tm, tn), lambda jp, i, jj, c=ecol0 // tn: (i, jp * pz + jj + c)))
        args.append(arr)
    return pl.pallas_call(
        functools.partial(_mmw_body, epilogue, len(extras), shift, transposed),
        grid=(n_tiles // pz, m // tm, pz),
        in_specs=in_specs,
        out_specs=pl.BlockSpec((tm, tn), lambda jp, i, jj: (i, jp * pz + jj)),
        out_shape=jax.ShapeDtypeStruct((m, n), out_dtype),
        scratch_shapes=[scr],
        compiler_params=_params(("parallel", "arbitrary", "arbitrary")),
        name=name,
    )(*args)


def _conv_body(k_width, zero_first, epilogue, n_extra, x_ref, prev_ref, w_ref, *refs):
    extra = refs[:n_extra]
    o_ref, buf = refs[-2], refs[-1]
    tr = x_ref.shape[0]
    prev = prev_ref[...]
    if zero_first:
        prev = jnp.where(pl.program_id(0) == 0, jnp.zeros_like(prev), prev)
    buf[0:SUBLANES, :] = prev
    buf[SUBLANES:SUBLANES + tr, :] = x_ref[...]
    w = w_ref[...]
    y = x_ref[...] * w[k_width - 1:k_width, :]
    for k in range(k_width - 1):
        y = y + buf[pl.ds(SUBLANES - (k_width - 1) + k, tr), :] * w[k:k + 1, :]
    o_ref[...] = epilogue(y, *[e[...] for e in extra]).astype(o_ref.dtype)


def _conv(x, prev, w, *, k_width, rows, row0, tr, tc, col0, n_cols, zero_first,
          epilogue, extras=(), out_dtype, name, out_rows=None, into=None):
    assert row0 % tr == 0 and rows % tr == 0 and col0 % tc == 0 and tr % SUBLANES == 0
    out_rows = rows if out_rows is None else out_rows
    o_off = 0 if into is None else row0 // tr
    r_off, c_off = row0 // tr, col0 // tc
    sub = tr // SUBLANES
    if prev is None:
        prev_arr = x
        prev_spec = pl.BlockSpec(
            (SUBLANES, tc), lambda i, j: (jnp.maximum((i + r_off) * sub - 1, 0), j + c_off))
    else:
        prev_arr = prev
        prev_spec = pl.BlockSpec((SUBLANES, tc), lambda i, j: (i, j))
    in_specs = [pl.BlockSpec((tr, tc), lambda i, j: (i + r_off, j + c_off)),
                prev_spec,
                pl.BlockSpec((k_width, tc), lambda i, j: (0, j))]
    args = [x, prev_arr, w]
    for arr, ecol0 in extras:
        assert ecol0 % tc == 0
        in_specs.append(pl.BlockSpec((tr, tc), lambda i, j, c=ecol0 // tc: (i + r_off, j + c)))
        args.append(arr)
    alias = _into(into, args, in_specs)
    if into is not None:
        assert into.shape[1] == n_cols and into.dtype == out_dtype
        out_rows = into.shape[0]
    return pl.pallas_call(
        functools.partial(_conv_body, k_width, zero_first, epilogue, len(extras)),
        grid=(rows // tr, pl.cdiv(n_cols, tc)),
        in_specs=in_specs,
        out_specs=pl.BlockSpec((tr, tc), lambda i, j: (i + o_off, j)),
        out_shape=jax.ShapeDtypeStruct((out_rows, n_cols), out_dtype),
        input_output_aliases=alias,
        scratch_shapes=[pltpu.VMEM((tr + SUBLANES, tc), F32)],
        compiler_params=_params(("parallel", "parallel")),
        name=name,
    )(*args)


def _dn_act(dn_width, y):
    sec = pl.program_id(1)
    y = _silu(y)
    scale = jnp.where(sec == 0, HEAD_DIM ** -0.5, 1.0).astype(F32)
    outs = []
    for h in range(dn_width // HEAD_DIM):
        yh = y[:, h * HEAD_DIM:(h + 1) * HEAD_DIM]
        ss = jnp.sum(yh * yh, axis=-1, keepdims=True)
        yn = yh * lax.rsqrt(ss + EPS) * scale
        outs.append(jnp.where(sec < 2, yn, yh))
    return jnp.concatenate(outs, axis=-1)


def _ffn_act(y, up):
    return _silu(y) * up


def _gate_body(chunk, ba_ref, alog_ref, dtb_ref, beta_ref, g_ref):
    x = ba_ref[...]
    tr = x.shape[0]
    beta_ref[...] = _sigmoid(x)
    z = x + dtb_ref[...]
    softplus = jnp.maximum(z, 0.0) + jnp.log1p(jnp.exp(-jnp.abs(z)))
    g = -jnp.exp(alog_ref[...]) * softplus
    i = lax.broadcasted_iota(jnp.int32, (tr, tr), 0)
    j = lax.broadcasted_iota(jnp.int32, (tr, tr), 1)
    tri = jnp.where((i // chunk == j // chunk) & (j <= i), 1.0, 0.0).astype(F32)
    g_ref[...] = jnp.dot(tri, g, preferred_element_type=F32, precision=lax.Precision.HIGHEST)


def _gates(ba, a_log, dt_bias, *, rows, row0, chunk, tr):
    n_heads = a_log.shape[0]
    pad = lambda v: jnp.zeros((1, LANES), F32).at[0, n_heads:2 * n_heads].set(v.astype(F32))
    off = row0 // tr
    assert row0 % tr == 0 and rows % tr == 0 and tr % chunk == 0
    return pl.pallas_call(
        functools.partial(_gate_body, chunk),
        grid=(rows // tr,),
        in_specs=[pl.BlockSpec((tr, LANES), lambda i: (i + off, 0)),
                  pl.BlockSpec((1, LANES), lambda i: (0, 0)),
                  pl.BlockSpec((1, LANES), lambda i: (0, 0))],
        out_specs=[pl.BlockSpec((tr, LANES), lambda i: (i, 0)),
                   pl.BlockSpec((tr, LANES), lambda i: (i, 0))],
        out_shape=[jax.ShapeDtypeStruct((rows, LANES), F32)] * 2,
        compiler_params=_params(("parallel",)),
        name="dn_gates",
    )(ba, pad(a_log), pad(dt_bias))


def _split3(x):
    hi = x.astype(BF16)
    return hi, (x - hi.astype(F32)).astype(BF16)


def _dot3(a, b):
    lhs = jnp.concatenate([a[0], a[0], a[1]], axis=1)
    rhs = jnp.concatenate([b[0], b[1], b[0]], axis=0)
    return jnp.dot(lhs, rhs, preferred_element_type=F32)


def _unit_lower_inverse(a_mats, ii, jj, chunk):
    base = min(chunk, 16)
    eye = jnp.where(ii == jj, 1.0, 0.0).astype(F32)
    in_base = ii // base == jj // base
    ds = [jnp.where(in_base, a, 0.0) for a in a_mats]
    ps = [eye - d for d in ds]
    dps = [_split3(d) for d in ds]
    for _ in range(int(math.log2(base)) - 1):
        dps = [_split3(_dot3(dp, dp)) for dp in dps]
        ps = [p + _dot3(_split3(p), dp) for p, dp in zip(ps, dps)]
    size = base
    while size < chunk:
        below = (ii // (2 * size) == jj // (2 * size)) & (ii // size != jj // size)
        es = [_split3(jnp.where(below, a, 0.0)) for a in a_mats]
        pss = [_split3(p) for p in ps]
        ts = [_split3(_dot3(p_s, e)) for p_s, e in zip(pss, es)]
        ps = [p - _dot3(t, p_s) for p, t, p_s in zip(ps, ts, pss)]
        size *= 2
    return ps


def _dn_local_body(chunk, n_heads, hb, q_ref, k_ref, v_ref, beta_ref, g_ref,
                   u_ref, w_ref, qd_ref, kdt_ref, aqk_ref, gl_ref):
    n = q_ref.shape[0]
    hg = pl.program_id(0)
    ii = lax.broadcasted_iota(jnp.int32, (n, n), 0)
    jj = lax.broadcasted_iota(jnp.int32, (n, n), 1)
    same = ii // chunk == jj // chunk
    causal = same & (ii >= jj)
    strict = same & (ii > jj)
    last = jj == (ii // chunk) * chunk + (chunk - 1)
    beta_all = beta_ref[...]
    g_all = g_ref[...]
    lane = lax.broadcasted_iota(jnp.int32, (n, LANES), 1)
    nt = (((1,), (1,)), ((), ()))
    a_mats, rhss = [], []
    for hh in range(hb):
        h = hg * hb + hh
        sl = slice(hh * HEAD_DIM, (hh + 1) * HEAD_DIM)
        q, k, v = q_ref[:, sl], k_ref[:, sl], v_ref[:, sl]
        beta = jnp.sum(jnp.where(lane == h, beta_all, 0.0), axis=-1, keepdims=True)
        gcum = jnp.sum(jnp.where(lane == h + n_heads, g_all, 0.0), axis=-1, keepdims=True)
        g_rows = jnp.broadcast_to(gcum, (n, n))
        g_cols = g_rows.T
        g_last = jnp.sum(jnp.where(last, g_cols, 0.0), axis=-1, keepdims=True)
        gamma = jnp.exp(jnp.where(causal, g_rows - g_cols, -jnp.inf))
        kb = k.astype(BF16)
        kk = lax.dot_general(kb, kb, nt, preferred_element_type=F32)
        qk = lax.dot_general(q.astype(BF16), kb, nt, preferred_element_type=F32)
        a_mats.append(jnp.where(strict, beta * kk * gamma, 0.0))
        eg = jnp.exp(gcum)
        rhss.append(_split3(jnp.concatenate([v * beta, k * (beta * eg)], axis=-1)))
        qd_ref[hh] = q * eg
        kdt_ref[hh] = (k * jnp.exp(g_last - gcum)).T
        aqk_ref[hh] = qk * gamma
        gl_ref[hh] = jnp.broadcast_to(jnp.exp(g_last), (n, LANES))
    invs = _unit_lower_inverse(a_mats, ii, jj, chunk)
    for hh in range(hb):
        sol = _dot3(_split3(invs[hh]), rhss[hh])
        u_ref[hh] = sol[:, :HEAD_DIM]
        w_ref[hh] = sol[:, HEAD_DIM:]


def _dn_local(qkv, beta, gcum, *, rows, row0, chunk, n_heads, dn_width, hb, n):
    assert rows % n == 0 and row0 % n == 0 and n % chunk == 0
    off = row0 // n
    bw = hb * HEAD_DIM
    sec = dn_width // bw
    qkv_spec = lambda s: pl.BlockSpec((n, bw), lambda g, i, s=s: (i + off, g + s * sec))
    head_spec = lambda: pl.BlockSpec((hb, n, HEAD_DIM), lambda g, i: (g, i, 0))
    gate_spec = pl.BlockSpec((n, LANES), lambda g, i: (i, 0))
    shp = lambda *s: jax.ShapeDtypeStruct(s, F32)
    return pl.pallas_call(
        functools.partial(_dn_local_body, chunk, n_heads, hb),
        grid=(n_heads // hb, rows // n),
        in_specs=[qkv_spec(0), qkv_spec(1), qkv_spec(2), gate_spec, gate_spec],
        out_specs=[head_spec(), head_spec(), head_spec(),
                   pl.BlockSpec((hb, HEAD_DIM, n), lambda g, i: (g, 0, i)),
                   pl.BlockSpec((hb, n, n), lambda g, i: (g, i, 0)),
                   head_spec()],
        out_shape=[shp(n_heads, rows, HEAD_DIM), shp(n_heads, rows, HEAD_DIM),
                   shp(n_heads, rows, HEAD_DIM), shp(n_heads, HEAD_DIM, rows),
                   shp(n_heads, rows, n), shp(n_heads, rows, LANES)],
        compiler_params=_params(("parallel", "parallel")),
        name="dn_local",
    )(qkv, qkv, qkv, beta, gcum)


def _gated_head_norm(o, z, gain):
    ms = jnp.mean(o * o, axis=-1, keepdims=True)
    return ((o * lax.rsqrt(ms + EPS)) * gain) * _silu(z)


def _dn_scan_body(chunk, n_heads, u_ref, w_ref, qd_ref, kdt_ref, aqk_ref, gl_ref, z_ref, gain_ref,
                  o_ref, s_out_ref, s_ref):
    step = pl.program_id(0)
    n = u_ref.shape[1]

    @pl.when(step == 0)
    def _():
        s_ref[...] = jnp.zeros_like(s_ref)

    gain = gain_ref[...]
    for c in range(n // chunk):
        rows = slice(c * chunk, (c + 1) * chunk)
        heads = range(n_heads)
        ps = [jnp.dot(jnp.concatenate([w_ref[h, rows, :], qd_ref[h, rows, :]], axis=0).astype(BF16),
                      s_ref[h].astype(BF16), preferred_element_type=F32) for h in heads]
        vbs = [(u_ref[h, rows, :] - ps[h][:chunk]).astype(BF16) for h in heads]
        for h in heads:
            s_ref[h] = s_ref[h] * gl_ref[h, c * chunk:c * chunk + 1, :] + jnp.dot(
                kdt_ref[h, :, rows].astype(BF16), vbs[h], preferred_element_type=F32)
        for h in heads:
            o = ps[h][chunk:] + jnp.dot(aqk_ref[h, rows, rows].astype(BF16), vbs[h],
                                        preferred_element_type=F32)
            cols = slice(h * HEAD_DIM, (h + 1) * HEAD_DIM)
            o_ref[rows, cols] = _gated_head_norm(o, z_ref[rows, cols], gain).astype(o_ref.dtype)

    @pl.when(step == pl.num_programs(0) - 1)
    def _():
        s_out_ref[...] = s_ref[...]


def _dn_scan(u, w, qd, kdt, aqk, gl, zsrc, z_col0, gain, *, chunk, n, out_rows):
    n_heads, rows, _ = u.shape
    dn_width = n_heads * HEAD_DIM
    assert z_col0 % dn_width == 0
    head_spec = lambda: pl.BlockSpec((n_heads, n, HEAD_DIM), lambda i: (0, i, 0))
    return pl.pallas_call(
        functools.partial(_dn_scan_body, chunk, n_heads),
        grid=(rows // n,),
        in_specs=[head_spec(), head_spec(), head_spec(),
                  pl.BlockSpec((n_heads, HEAD_DIM, n), lambda i: (0, 0, i)),
                  pl.BlockSpec((n_heads, n, n), lambda i: (0, i, 0)),
                  head_spec(),
                  pl.BlockSpec((n, dn_width), lambda i: (i, z_col0 // dn_width)),
                  pl.BlockSpec((1, HEAD_DIM), lambda i: (0, 0))],
        out_specs=[pl.BlockSpec((n, dn_width), lambda i: (i, 0)),
                   pl.BlockSpec((n_heads, HEAD_DIM, HEAD_DIM), lambda i: (0, 0, 0))],
        out_shape=[jax.ShapeDtypeStruct((out_rows, dn_width), BF16),
                   jax.ShapeDtypeStruct((n_heads, HEAD_DIM, HEAD_DIM), F32)],
        scratch_shapes=[pltpu.VMEM((n_heads, HEAD_DIM, HEAD_DIM), F32)],
        compiler_params=_params(("arbitrary",)),
        name="dn_scan",
    )(u, w, qd, kdt, aqk, gl, zsrc, gain.reshape(1, HEAD_DIM))


def _dn_sample_body(t_len, u_ref, w_ref, qd_ref, kdt_ref, aqk_ref, gl_ref, z_ref, gain_ref, s_ref,
                    into_ref, o_ref, s_out_ref):
    n = u_ref.shape[1]
    nb = n // t_len
    s = s_ref[0, :, 0]
    w3 = w_ref[0].reshape(nb, t_len, HEAD_DIM)
    q3 = qd_ref[0].reshape(nb, t_len, HEAD_DIM)
    wq = jnp.concatenate([w3, q3], axis=1).astype(BF16)
    p = jnp.einsum('bck,bkd->bcd', wq, s.astype(BF16), preferred_element_type=F32)
    ws = p[:, :t_len].reshape(n, HEAD_DIM)
    qs = p[:, t_len:].reshape(n, HEAD_DIM)
    v_new = u_ref[0] - ws
    vb = v_new.astype(BF16)
    o = qs + jnp.dot(aqk_ref[0].astype(BF16), vb, preferred_element_type=F32)
    kdt = kdt_ref[0]
    ri = lax.broadcasted_iota(jnp.int32, (nb * HEAD_DIM, n), 0)
    ci = lax.broadcasted_iota(jnp.int32, (nb * HEAD_DIM, n), 1)
    zt = jnp.where(ri // HEAD_DIM == ci // t_len, jnp.tile(kdt, (nb, 1)), 0.0).astype(BF16)
    upd = jnp.dot(zt, vb, preferred_element_type=F32).reshape(nb, HEAD_DIM, HEAD_DIM)
    gl = gl_ref[0].reshape(nb, t_len, LANES)[:, 0:1, :]
    s_out_ref[0, :, 0] = s * gl + upd
    o_ref[...] = _gated_head_norm(o, z_ref[...], gain_ref[...]).astype(o_ref.dtype)


def _dn_sample(u, w, qd, kdt, aqk, gl, zsrc, z_row0, z_col0, gain, state, into, *, t_len):
    n_heads, n, _ = u.shape
    nb = n // t_len
    assert z_row0 % n == 0 and z_col0 % HEAD_DIM == 0
    head_spec = lambda: pl.BlockSpec((1, n, HEAD_DIM), lambda h: (h, 0, 0))
    state_spec = pl.BlockSpec((1, nb, 1, HEAD_DIM, HEAD_DIM), lambda h: (0, 0, h, 0, 0))
    in_specs = [head_spec(), head_spec(), head_spec(),
                pl.BlockSpec((1, HEAD_DIM, n), lambda h: (h, 0, 0)),
                pl.BlockSpec((1, n, n), lambda h: (h, 0, 0)),
                head_spec(),
                pl.BlockSpec((n, HEAD_DIM), lambda h: (z_row0 // n, z_col0 // HEAD_DIM + h)),
                pl.BlockSpec((1, HEAD_DIM), lambda h: (0, 0)),
                state_spec]
    args = [u, w, qd, kdt, aqk, gl, zsrc, gain.reshape(1, HEAD_DIM), state]
    alias = _into(into, args, in_specs)
    return pl.pallas_call(
        functools.partial(_dn_sample_body, t_len),
        grid=(n_heads,),
        in_specs=in_specs,
        out_specs=[pl.BlockSpec((n, HEAD_DIM), lambda h: (z_row0 // n, h)), state_spec],
        out_shape=[jax.ShapeDtypeStruct(into.shape, into.dtype),
                   jax.ShapeDtypeStruct(state.shape, F32)],
        input_output_aliases=alias,
        compiler_params=_params(("parallel",)),
        name="dn_sample",
    )(*args)


def _rope_body(n_heads, dil, q_ref, k_ref, v_ref, cos_ref, sin_ref, qo_ref, kvo_ref, slab):
    width = n_heads * HEAD_DIM
    per = q_ref.shape[0] // dil
    cos, sin = cos_ref[...], sin_ref[...]

    def regroup(x, dst_ref, dst_sl):
        if dil == 1:
            dst_ref[0, :, dst_sl] = x.astype(dst_ref.dtype)
            return
        slab[...] = x

        def one_residue(r, carry):
            dst_ref[r, :, dst_sl] = slab[pl.ds(r, per, stride=dil), :].astype(dst_ref.dtype)
            return carry

        lax.fori_loop(0, dil, one_residue, 0)

    for h in range(n_heads):
        sl = slice(h * HEAD_DIM, (h + 1) * HEAD_DIM)
        q = q_ref[:, sl]
        k = k_ref[:, sl]
        regroup(q * cos + pltpu.roll(q, HEAD_DIM // 2, 1) * sin, qo_ref, sl)
        regroup(k * cos + pltpu.roll(k, HEAD_DIM // 2, 1) * sin, kvo_ref, sl)
        regroup(v_ref[:, sl], kvo_ref, slice(width + h * HEAD_DIM, width + (h + 1) * HEAD_DIM))


def _rope(att_src, col0, cos, sin, *, rows, row0, n_heads, tr, dil, q_dtype):
    width = n_heads * HEAD_DIM
    assert row0 % tr == 0 and rows % tr == 0 and col0 % width == 0 and tr % (16 * dil) == 0
    r_off, c_off = row0 // tr, col0 // width
    per = tr // dil
    src = lambda s: pl.BlockSpec((tr, width), lambda i, s=s: (i + r_off, c_off + s))
    tab = pl.BlockSpec((tr, HEAD_DIM), lambda i: (i, 0))
    return pl.pallas_call(
        functools.partial(_rope_body, n_heads, dil),
        grid=(rows // tr,),
        in_specs=[src(0), src(1), src(2), tab, tab],
        out_specs=[pl.BlockSpec((dil, per, width), lambda i: (0, i, 0)),
                   pl.BlockSpec((dil, per, 2 * width), lambda i: (0, i, 0))],
        out_shape=[jax.ShapeDtypeStruct((dil, rows // dil, width), q_dtype),
                   jax.ShapeDtypeStruct((dil, rows // dil, 2 * width), F32)],
        scratch_shapes=[pltpu.VMEM((tr, HEAD_DIM), F32)],
        compiler_params=_params(("parallel",)),
        name="rope",
    )(att_src, att_src, att_src, cos, sin)


def _rope_tables(pos):
    half = HEAD_DIM // 2
    inv_freq = ROPE_THETA ** (-jnp.arange(half, dtype=F32) / half)
    ang = pos.astype(F32)[:, None] * inv_freq[None, :]
    cos, sin = jnp.cos(ang), jnp.sin(ang)
    return jnp.concatenate([cos, cos], axis=-1), jnp.concatenate([-sin, sin], axis=-1)


def _attn_prompt_body(n_heads, dil, q_ref, kc_ref, kp_ref, vc_ref, vp_ref, o_ref, lse_ref):
    nb = ATT_BLOCK
    blk = pl.program_id(0)
    r = pl.program_id(1)
    rows = pl.ds(r, nb, stride=dil) if dil > 1 else slice(None)
    i = lax.broadcasted_iota(jnp.int32, (nb, 2 * nb), 0)
    j = lax.broadcasted_iota(jnp.int32, (nb, 2 * nb), 1)
    mask = (j >= i) & (j <= i + nb) & ((blk > 0) | (j >= nb))
    lane = lax.broadcasted_iota(jnp.int32, (nb, LANES), 1)
    scale = HEAD_DIM ** -0.5
    lse_all = jnp.zeros((nb, LANES), F32)
    for h in range(n_heads):
        sl = slice(h * HEAD_DIM, (h + 1) * HEAD_DIM)
        k = jnp.concatenate([kp_ref[:, sl], kc_ref[:, sl]], axis=0).astype(BF16)
        v = jnp.concatenate([vp_ref[:, sl], vc_ref[:, sl]], axis=0).astype(BF16)
        s = lax.dot_general(q_ref[:, sl], k, (((1,), (1,)), ((), ())), preferred_element_type=F32) * scale
        s = jnp.where(mask, s, -jnp.inf)
        m = jnp.max(s, axis=-1, keepdims=True)
        p = jnp.exp(s - m)
        l = jnp.sum(p, axis=-1, keepdims=True)
        o_ref[h, rows, :] = jnp.dot(p.astype(BF16), v, preferred_element_type=F32) / l
        lse_all = jnp.where(lane == h, m + jnp.log(l), lse_all)
    lse_ref[rows, :] = lse_all


def _attn_prompt(q, kv, dil, n_heads):
    _, m_len, width = q.shape
    nb = ATT_BLOCK
    t_len = m_len * dil
    assert m_len % nb == 0
    src = lambda f: pl.BlockSpec((None, nb, width), f)
    prev = lambda n: jnp.maximum(n - 1, 0)
    return pl.pallas_call(
        functools.partial(_attn_prompt_body, n_heads, dil),
        grid=(m_len // nb, dil),
        in_specs=[src(lambda n, r: (r, n, 0)),
                  src(lambda n, r: (r, n, 0)), src(lambda n, r: (r, prev(n), 0)),
                  src(lambda n, r: (r, n, 1)), src(lambda n, r: (r, prev(n), 1))],
        out_specs=[pl.BlockSpec((n_heads, nb * dil, HEAD_DIM), lambda n, r: (0, n, 0)),
                   pl.BlockSpec((nb * dil, LANES), lambda n, r: (n, 0))],
        out_shape=[jax.ShapeDtypeStruct((n_heads, t_len, HEAD_DIM), F32),
                   jax.ShapeDtypeStruct((t_len, LANES), F32)],
        compiler_params=_params(("parallel", "arbitrary")),
        name="attn_prompt",
    )(q, kv, kv, kv, kv)


def _attn_sample_body(n_heads, dil, stride, q_ref, kvn_ref, cache_ref, o_ref, lse_ref):
    t_len = q_ref.shape[1]
    width = n_heads * HEAD_DIM
    n_buf = math.prod(cache_ref.shape[1:-3])
    n_pair = n_buf * n_heads
    lg_h, lg_t = int(math.log2(n_heads)), int(math.log2(t_len))
    assert (1 << lg_h) == n_heads and (1 << lg_t) == t_len and dil & (dil - 1) == 0

    def cache_part(kv):
        if stride > 1:
            x = cache_ref[0, :, :, kv]
        else:
            x = cache_ref[0, :, kv]
        return x.reshape(n_pair, HEAD_DIM).astype(BF16)

    def heads_on_rows(ref, col0):
        return jnp.concatenate(
            [ref[0, :, col0 + h * HEAD_DIM:col0 + (h + 1) * HEAD_DIM] for h in range(n_heads)],
            axis=0).astype(BF16)

    q_all = heads_on_rows(q_ref, 0)
    k_new = heads_on_rows(kvn_ref, 0)
    v_new = heads_on_rows(kvn_ref, width)
    rq = lax.broadcasted_iota(jnp.int32, (n_heads * t_len, n_pair), 0)
    cc = lax.broadcasted_iota(jnp.int32, (n_heads * t_len, n_pair), 1)
    jq, hq = rq & (t_len - 1), rq >> lg_t
    row, hc = cc >> lg_h, cc & (n_heads - 1)
    if stride > 1:
        idx = (row >> lg_t) * stride + (row & (t_len - 1))
    else:
        idx = row
    mask_buf = (hq == hc) & (idx >= jq) & (((idx - jq) & (dil - 1)) == 0)
    rn = lax.broadcasted_iota(jnp.int32, (n_heads * t_len, n_heads * t_len), 0)
    cn = lax.broadcasted_iota(jnp.int32, (n_heads * t_len, n_heads * t_len), 1)
    jn, jc = rn & (t_len - 1), cn & (t_len - 1)
    mask_new = ((rn >> lg_t) == (cn >> lg_t)) & (jc <= jn) & (((jn - jc) & (dil - 1)) == 0)
    scale = HEAD_DIM ** -0.5
    nt = (((1,), (1,)), ((), ()))
    s_buf = lax.dot_general(q_all, cache_part(0), nt, preferred_element_type=F32) * scale
    s_new = lax.dot_general(q_all, k_new, nt, preferred_element_type=F32) * scale
    s_buf = jnp.where(mask_buf, s_buf, -jnp.inf)
    s_new = jnp.where(mask_new, s_new, -jnp.inf)
    m = jnp.maximum(jnp.max(s_buf, axis=-1, keepdims=True), jnp.max(s_new, axis=-1, keepdims=True))
    p_buf = jnp.exp(s_buf - m)
    p_new = jnp.exp(s_new - m)
    l = jnp.sum(p_buf, axis=-1, keepdims=True) + jnp.sum(p_new, axis=-1, keepdims=True)
    o = (jnp.dot(p_buf.astype(BF16), cache_part(1), preferred_element_type=F32)
         + jnp.dot(p_new.astype(BF16), v_new, preferred_element_type=F32)) / l
    lse = m + jnp.log(l)
    lane = lax.broadcasted_iota(jnp.int32, (t_len, LANES), 1)
    lse_all = jnp.zeros((t_len, LANES), F32)
    for h in range(n_heads):
        o_ref[h] = o[h * t_len:(h + 1) * t_len]
        lse_all = jnp.where(lane == h, lse[h * t_len:(h + 1) * t_len], lse_all)
    lse_ref[...] = lse_all


def _attn_sample(q, kvn, cache, win, dil, n_heads, t_len):
    bsz, buf_len = cache.shape[:2]
    width = n_heads * HEAD_DIM
    assert buf_len == win and win == ATT_BLOCK * dil, "window buffer must be full"
    if dil >= 2 * t_len:
        stride = dil
        cache_v = cache.reshape(bsz, buf_len // dil, dil, 2, n_heads, HEAD_DIM)
        cache_spec = pl.BlockSpec((1, buf_len // dil, t_len, 2, n_heads, HEAD_DIM),
                                  lambda b: (b, 0, 0, 0, 0, 0))
    else:
        stride = 1
        cache_v = cache
        cache_spec = pl.BlockSpec((1, buf_len, 2, n_heads, HEAD_DIM), lambda b: (b, 0, 0, 0, 0))
    return pl.pallas_call(
        functools.partial(_attn_sample_body, n_heads, dil, stride),
        grid=(bsz,),
        in_specs=[pl.BlockSpec((1, t_len, width), lambda b: (0, b, 0)),
                  pl.BlockSpec((1, t_len, 2 * width), lambda b: (0, b, 0)),
                  cache_spec],
        out_specs=[pl.BlockSpec((n_heads, t_len, HEAD_DIM), lambda b: (0, b, 0)),
                   pl.BlockSpec((t_len, LANES), lambda b: (b, 0))],
        out_shape=[jax.ShapeDtypeStruct((n_heads, bsz * t_len, HEAD_DIM), F32),
                   jax.ShapeDtypeStruct((bsz * t_len, LANES), F32)],
        compiler_params=_params(("parallel",)),
        name="attn_sample",
    )(q, kvn, cache_v)


def _merge_body(n_heads, o0, o1, o2, l0, l1, l2, *refs):
    out_ref = refs[-1]
    la, lb, lc = l0[...], l1[...], l2[...]
    m = jnp.maximum(jnp.maximum(la, lb), lc)
    ea, eb, ec = jnp.exp(la - m), jnp.exp(lb - m), jnp.exp(lc - m)
    tot = ea + eb + ec
    wa, wb, wc = ea / tot, eb / tot, ec / tot
    for h in range(n_heads):
        sl = slice(h * HEAD_DIM, (h + 1) * HEAD_DIM)
        col = slice(h, h + 1)
        out_ref[:, sl] = (wa[:, col] * o0[h] + wb[:, col] * o1[h]
                          + wc[:, col] * o2[h]).astype(out_ref.dtype)


def _merge(outs, lses, tr, *, out_rows=None, row0=0, into=None):
    n_heads, rows, _ = outs[0].shape
    width = n_heads * HEAD_DIM
    assert row0 % tr == 0 and rows % tr == 0
    off = row0 // tr
    spec = pl.BlockSpec((n_heads, tr, HEAD_DIM), lambda i: (0, i, 0))
    lspec = pl.BlockSpec((tr, LANES), lambda i: (i, 0))
    in_specs = [spec] * 3 + [lspec] * 3
    args = [*outs, *lses]
    alias = _into(into, args, in_specs)
    out_rows = into.shape[0] if into is not None else (rows if out_rows is None else out_rows)
    return pl.pallas_call(
        functools.partial(_merge_body, n_heads),
        grid=(rows // tr,),
        in_specs=in_specs,
        out_specs=pl.BlockSpec((tr, width), lambda i: (i + off, 0)),
        out_shape=jax.ShapeDtypeStruct((out_rows, width), BF16),
        input_output_aliases=alias,
        compiler_params=_params(("parallel",)),
        name="merge_groups",
    )(*args)


def _layer(x, t_p, bsz, t_s, past_len, caches, dn_state, dn_conv_state, ffn_conv_state,
           norm_mix, w_in, dn_conv_w, dn_a_log, dn_dt_bias, dn_out_norm, w_branch_dn,
           w_branch_att, w_out, norm_ffn, w_ffn_gate, w_ffn_up, ffn_conv_w, w_ffn_down):
    r_all, d_model = x.shape
    n_s = bsz * t_s
    n_heads_dn = dn_a_log.shape[0]
    dn_width = n_heads_dn * HEAD_DIM
    n_groups = len(ATT_GROUPS)
    att_width = w_branch_att.shape[0]
    n_heads_att = att_width // HEAD_DIM
    d_ff = w_ffn_gate.shape[1]
    tm = _pick_tile(r_all, 1100, 16)

    c_ba = 4 * dn_width
    c_att = c_ba + 2 * n_heads_dn
    n1 = _rmsnorm(x, norm_mix, BF16)
    w_in_t = w_in.T
    proj = functools.partial(_matmul_w, n1, w_in_t, transposed=True, out_dtype=F32, tm=tm)
    qkvz = proj(col0=0, n=c_ba, tn=512, name="proj_dn")
    ba = proj(col0=c_ba, n=LANES, tn=LANES, name="proj_ba")
    rest = proj(col0=c_att, tn=512, name="proj_att")
    col_gdn = 3 * n_groups * att_width
    col_gatt = col_gdn + d_model

    dn_act = functools.partial(_dn_act, dn_width)
    conv_args = dict(k_width=DN_CONV, tc=dn_width, col0=0, n_cols=3 * dn_width,
                     epilogue=dn_act, out_dtype=F32)
    qkv_p = _conv(qkvz, None, dn_conv_w, rows=t_p, row0=0, tr=256, zero_first=True,
                  name="dn_conv_prompt", **conv_args)
    state_pad = jnp.pad(dn_conv_state, ((0, 0), (SUBLANES - (DN_CONV - 1), 0), (0, 0)))
    qkv_s = _conv(qkvz, state_pad.reshape(bsz * SUBLANES, -1), dn_conv_w, rows=n_s, row0=t_p,
                  tr=t_s, zero_first=False, name="dn_conv_sample", **conv_args)
    p_dn_conv = qkvz[t_p - (DN_CONV - 1):t_p, :3 * dn_width][None]
    s_dn_conv = qkvz[t_p:].reshape(bsz, t_s, -1)[:, t_s - (DN_CONV - 1):, :3 * dn_width]

    beta_p, g_p = _gates(ba, dn_a_log, dn_dt_bias, rows=t_p, row0=0, chunk=DN_CHUNK, tr=512)
    beta_s, g_s = _gates(ba, dn_a_log, dn_dt_bias, rows=n_s, row0=t_p, chunk=t_s, tr=n_s)

    loc_p = _dn_local(qkv_p, beta_p, g_p, rows=t_p, row0=0, chunk=DN_CHUNK,
                      n_heads=n_heads_dn, dn_width=dn_width, hb=8, n=128)
    o_dn, p_dn = _dn_scan(*loc_p, qkvz, 3 * dn_width, dn_out_norm, chunk=DN_CHUNK, n=128,
                          out_rows=r_all)
    loc_s = _dn_local(qkv_s, beta_s, g_s, rows=n_s, row0=0, chunk=t_s,
                      n_heads=n_heads_dn, dn_width=dn_width, hb=4, n=n_s)
    o_dn, s_dn = _dn_sample(*loc_s, qkvz, t_p, 3 * dn_width, dn_out_norm, dn_state, o_dn, t_len=t_s)

    cos_p, sin_p = _rope_tables(jnp.arange(t_p, dtype=jnp.int32))
    cos_s, sin_s = _rope_tables(past_len + jnp.tile(jnp.arange(t_s, dtype=jnp.int32), bsz))
    outs_p, lses_p, outs_s, lses_s, p_kv, s_kv = [], [], [], [], [], []
    for gi, (win, dil) in enumerate(ATT_GROUPS):
        col = 3 * gi * att_width
        q_p, kv_p = _rope(rest, col, cos_p, sin_p, rows=t_p, row0=0, n_heads=n_heads_att,
                          tr=512, dil=dil, q_dtype=BF16)
        q_s, kv_s = _rope(rest, col, cos_s, sin_s, rows=n_s, row0=t_p, n_heads=n_heads_att,
                          tr=n_s, dil=1, q_dtype=F32)
        o, lse = _attn_prompt(q_p, kv_p, dil, n_heads_att)
        outs_p.append(o)
        lses_p.append(lse)
        o, lse = _attn_sample(q_s, kv_s, caches[gi], win, dil, n_heads_att, t_s)
        outs_s.append(o)
        lses_s.append(lse)
        keep = min(win, t_p)
        tail = kv_p[:, (t_p - keep) // dil:].transpose(1, 0, 2)
        p_kv.append(tail.reshape(1, keep, 2, n_heads_att, HEAD_DIM))
        s_kv.append(kv_s.reshape(bsz, t_s, 2, n_heads_att, HEAD_DIM))
    o_att = _merge(outs_p, lses_p, 256, out_rows=r_all)
    o_att = _merge(outs_s, lses_s, n_s, row0=t_p, into=o_att)

    y_dn = _matmul_w(o_dn, w_branch_dn, out_dtype=F32, tm=tm, tn=512,
                     epilogue=lambda acc, g: _sigmoid(g) * acc, extras=[(rest, col_gdn)],
                     name="branch_dn")
    mix = _matmul_w(o_att, w_branch_att, out_dtype=BF16, tm=tm, tn=512,
                    epilogue=lambda acc, g, y: y + _sigmoid(g) * acc,
                    extras=[(rest, col_gatt), (y_dn, 0)], name="branch_att")
    x1 = _matmul_w(mix, w_out, out_dtype=F32, tm=tm, tn=512,
                   epilogue=lambda acc, r: r + acc, extras=[(x, 0)], name="out_proj")

    n2 = _rmsnorm(x1, norm_ffn, BF16)
    gate = _matmul_w(n2, w_ffn_gate, out_dtype=F32, tm=tm, tn=512, name="ffn_gate")
    up = _matmul_w(n2, w_ffn_up, out_dtype=F32, tm=tm, tn=512, name="ffn_up")
    ffn_args = dict(k_width=FFN_CONV, tc=1024, col0=0, n_cols=d_ff, epilogue=_ffn_act,
                    extras=[(up, 0)], out_dtype=BF16)
    h = _conv(gate, None, ffn_conv_w, rows=t_p, row0=0, tr=512, zero_first=True,
              name="ffn_conv_prompt", out_rows=r_all, **ffn_args)
    fstate_pad = jnp.pad(ffn_conv_state, ((0, 0), (SUBLANES - (FFN_CONV - 1), 0), (0, 0)))
    ffn_args["tc"] = d_ff
    h = _conv(gate, fstate_pad.reshape(bsz * SUBLANES, -1), ffn_conv_w, rows=n_s, row0=t_p,
              tr=t_s, zero_first=False, name="ffn_conv_sample", into=h, **ffn_args)
    p_ffn_conv = gate[t_p - (FFN_CONV - 1):t_p][None]
    s_ffn_conv = gate[t_p:].reshape(bsz, t_s, d_ff)[:, t_s - (FFN_CONV - 1):]
    tm_down = _pick_tile(r_all, 600, 16)
    x2 = _matmul(h, w_ffn_down.astype(BF16), out_dtype=F32, tm=tm_down, tn=256,
                 epilogue=lambda acc, r: r + acc, extras=[(x1, 0)], name="ffn_down")
    states_p = (p_kv[0], p_kv[1], p_kv[2], p_dn[None], p_dn_conv, p_ffn_conv)
    states_s = (s_kv[0], s_kv[1], s_kv[2], s_dn[0], s_dn_conv, s_ffn_conv)
    return x2, states_p, states_s


def kernel(x_prompt, x_sample, cache_kv_w128, cache_kv_w512, cache_kv_w2048, state_dn, state_dn_conv, state_ffn_conv, norm_mix, w_in, dn_conv_w, dn_a_log, dn_dt_bias, dn_out_norm, w_branch_dn, w_branch_att, w_out, norm_ffn, w_ffn_gate, w_ffn_up, ffn_conv_w, w_ffn_down, norm_final):
    b_p, t_p, d_model = x_prompt.shape
    bsz, t_s, _ = x_sample.shape
    depth = w_in.shape[0]
    assert b_p == 1 and depth == 1, "one prompt sequence, one layer"
    x = jnp.concatenate([x_prompt.reshape(t_p, d_model), x_sample.reshape(bsz * t_s, d_model)], axis=0)
    l = 0
    x, st_p, st_s = _layer(
        x, t_p, bsz, t_s, PAST_LEN,
        (cache_kv_w128[l], cache_kv_w512[l], cache_kv_w2048[l]),
        state_dn[l:l + 1], state_dn_conv[l], state_ffn_conv[l],
        norm_mix[l], w_in[l], dn_conv_w[l], dn_a_log[l], dn_dt_bias[l], dn_out_norm[l],
        w_branch_dn[l], w_branch_att[l], w_out[l], norm_ffn[l], w_ffn_gate[l], w_ffn_up[l],
        ffn_conv_w[l], w_ffn_down[l])
    y_prompt = _rmsnorm(x, norm_final, F32, row0=0, rows=t_p).reshape(1, t_p, d_model)
    y_sample = _rmsnorm(x, norm_final, F32, row0=t_p, rows=bsz * t_s).reshape(bsz, t_s, d_model)
    return (y_prompt, y_sample) + tuple(s[None] for s in st_p) + tuple(s[None] for s in st_s)
```

```python
import functools
import math

import jax
import jax.numpy as jnp
from jax import lax
from jax.experimental import pallas as pl
from jax.experimental.pallas import tpu as pltpu

F32 = jnp.float32
BF16 = jnp.bfloat16

EPS = 1e-6
ROPE_THETA = 10000.0
HEAD_DIM = 128
DN_CHUNK = 64
DN_CONV = 4
FFN_CONV = 3
ATT_GROUPS = ((128, 1), (512, 4), (2048, 16))
ATT_BLOCK = 128
PAST_LEN = 8192
SUBLANES = 8
LANES = 128
VMEM_LIMIT = 60 * 1024 * 1024


def _pick_tile(n, target, mult):
    best = None
    for t in range(mult, min(n, target) + 1, mult):
        if n % t == 0:
            best = t
    assert best is not None, (n, target, mult)
    return best


def _params(sem):
    return pltpu.CompilerParams(dimension_semantics=sem, vmem_limit_bytes=VMEM_LIMIT)


def _sigmoid(x):
    return 1.0 / (1.0 + jnp.exp(-x))


def _silu(x):
    return x * _sigmoid(x)


def _into(into, args, in_specs):
    if into is None:
        return {}
    in_specs.append(pl.BlockSpec(memory_space=pl.ANY))
    args.append(into)
    return {len(args) - 1: 0}


def _rmsnorm_body(x_ref, g_ref, *refs):
    o_ref = refs[-1]
    x = x_ref[...]
    ms = jnp.mean(x * x, axis=-1, keepdims=True)
    o_ref[...] = ((x * lax.rsqrt(ms + EPS)) * g_ref[...]).astype(o_ref.dtype)


def _rmsnorm(x, gain, out_dtype, *, row0=0, rows=None, tile=512, out_rows=None, out_row0=0, into=None):
    d = x.shape[1]
    rows = x.shape[0] if rows is None else rows
    tr = _pick_tile(math.gcd(math.gcd(rows, row0), out_row0), tile, 16)
    off, o_off = row0 // tr, out_row0 // tr
    in_specs = [pl.BlockSpec((tr, d), lambda i: (i + off, 0)),
                pl.BlockSpec((1, d), lambda i: (0, 0))]
    args = [x, gain.reshape(1, d)]
    alias = _into(into, args, in_specs)
    out_rows = into.shape[0] if into is not None else (rows if out_rows is None else out_rows)
    return pl.pallas_call(
        _rmsnorm_body,
        grid=(rows // tr,),
        in_specs=in_specs,
        out_specs=pl.BlockSpec((tr, d), lambda i: (i + o_off, 0)),
        out_shape=jax.ShapeDtypeStruct((out_rows, d), out_dtype),
        input_output_aliases=alias,
        compiler_params=_params(("parallel",)),
        name="rmsnorm",
    )(*args)


def _mm_body(epilogue, n_extra, a_ref, b_ref, *refs):
    extra = refs[:n_extra]
    o_ref = refs[n_extra]
    acc = jnp.dot(a_ref[...], b_ref[...], preferred_element_type=F32)
    o_ref[...] = epilogue(acc, *[e[...] for e in extra]).astype(o_ref.dtype)


def _matmul(a, b, *, out_dtype, tm, tn, epilogue=None, extras=(), name="matmul"):
    m, k = a.shape
    n = b.shape[1]
    if epilogue is None:
        epilogue = lambda acc: acc
    in_specs = [pl.BlockSpec((tm, k), lambda i, j: (i, 0)),
                pl.BlockSpec((k, tn), lambda i, j: (0, j))]
    args = [a, b]
    for arr, col0 in extras:
        assert col0 % tn == 0
        in_specs.append(pl.BlockSpec((tm, tn), lambda i, j, c=col0 // tn: (i, j + c)))
        args.append(arr)
    return pl.pallas_call(
        functools.partial(_mm_body, epilogue, len(extras)),
        grid=(m // tm, pl.cdiv(n, tn)),
        in_specs=in_specs,
        out_specs=pl.BlockSpec((tm, tn), lambda i, j: (i, j)),
        out_shape=jax.ShapeDtypeStruct((m, n), out_dtype),
        compiler_params=_params(("parallel", "parallel")),
        name=name,
    )(*args)


def _mmw_body(epilogue, n_extra, shift, transposed, a_ref, b_ref, *refs):
    if shift:
        bn_ref, refs = refs[0], refs[1:]
    extra = refs[:n_extra]
    o_ref, w_scr = refs[n_extra], refs[n_extra + 1]
    slot = pl.program_id(2)

    @pl.when(pl.program_id(1) == 0)
    def _():
        n_rows = b_ref.shape[0]
        step = min(n_rows, 256)
        for c in range(0, n_rows, step):
            rows = slice(c, c + step)
            if not shift:
                w = b_ref[rows, :]
            elif not transposed:
                w = jnp.concatenate([b_ref[rows, shift:], bn_ref[rows, :shift]], axis=1)
            elif c + step < n_rows:
                w = b_ref[c + shift:c + step + shift, :]
            else:
                w = jnp.concatenate([b_ref[c + shift:, :], bn_ref[...]], axis=0)
            w_scr[slot, rows, :] = w.astype(BF16)

    dims = (((1,), (1,)), ((), ())) if transposed else (((1,), (0,)), ((), ()))
    acc = lax.dot_general(a_ref[...], w_scr[slot], dims, preferred_element_type=F32)
    o_ref[...] = epilogue(acc, *[e[...] for e in extra]).astype(o_ref.dtype)


def _matmul_w(a, b, *, col0=0, n=None, transposed=False, out_dtype, tm, tn, epilogue=None,
              extras=(), name="matmul_w", row0=0, rows=None):
    k = a.shape[1]
    m = a.shape[0] if rows is None else rows
    assert row0 % tm == 0 and m % tm == 0
    r_off = row0 // tm
    n_total = b.shape[0] if transposed else b.shape[1]
    n = n_total - col0 if n is None else n
    shift = col0 % tn if transposed else col0 % LANES
    assert (col0 - shift) % tn == 0 and tn % LANES == 0
    if transposed and shift:
        assert tn % shift == 0 and shift % SUBLANES == 0
    jb = (col0 - shift) // tn
    n_tiles = pl.cdiv(n, tn)
    pz = 2 if n_tiles % 2 == 0 else 1
    if epilogue is None:
        epilogue = lambda acc: acc

    def tile(jp, i, jj):
        return jp * pz + jnp.where(i == 0, jj, pz - 1) + jb

    if transposed:
        b_spec = pl.BlockSpec((tn, k), lambda jp, i, jj: (tile(jp, i, jj), 0))
        bn_spec = pl.BlockSpec((shift or SUBLANES, k),
                               lambda jp, i, jj: ((tile(jp, i, jj) + 1) * (tn // (shift or tn)), 0))
        scr = pltpu.VMEM((pz, tn, k), BF16)
    else:
        b_spec = pl.BlockSpec((k, tn), lambda jp, i, jj: (0, tile(jp, i, jj)))
        bn_spec = pl.BlockSpec((k, LANES), lambda jp, i, jj: (0, (tile(jp, i, jj) + 1) * (tn // LANES)))
        scr = pltpu.VMEM((pz, k, tn), BF16)
    in_specs = [pl.BlockSpec((tm, k), lambda jp, i, jj: (i + r_off, 0)), b_spec]
    args = [a, b]
    if shift:
        in_specs.append(bn_spec)
        args.append(b)
    for arr, ecol0 in extras:
        assert ecol0 % tn == 0
        e_rows = min(tm, arr.shape[0])
        in_specs.append(pl.BlockSpec(
            (e_rows, tn), lambda jp, i, jj, c=ecol0 // tn, s=int(e_rows == tm): (i * s, jp * pz + jj + c)))
        args.append(arr)
    return pl.pallas_call(
        functools.partial(_mmw_body, epilogue, len(extras), shift, transposed),
        grid=(n_tiles // pz, m // tm, pz),
        in_specs=in_specs,
        out_specs=pl.BlockSpec((tm, tn), lambda jp, i, jj: (i, jp * pz + jj)),
        out_shape=jax.ShapeDtypeStruct((m, n), out_dtype),
        scratch_shapes=[scr],
        compiler_params=_params(("parallel", "arbitrary", "arbitrary")),
        name=name,
    )(*args)


def _ffn_up_body(k_width, a_ref, b_ref, g_ref, gprev_ref, cw_ref, o_ref, w_scr, buf):
    i, slot = pl.program_id(1), pl.program_id(2)

    @pl.when(i == 0)
    def _():
        step = min(b_ref.shape[0], 256)
        for c in range(0, b_ref.shape[0], step):
            w_scr[slot, c:c + step, :] = b_ref[c:c + step, :].astype(BF16)

    tm = a_ref.shape[0]
    acc = jnp.dot(a_ref[...], w_scr[slot], preferred_element_type=F32)
    prev = gprev_ref[...]
    buf[0:SUBLANES, :] = jnp.where(i == 0, jnp.zeros_like(prev), prev)
    buf[SUBLANES:SUBLANES + tm, :] = g_ref[...]
    cw = cw_ref[...]
    rc = _pick_tile(tm, 128, 16)
    for c in range(0, tm, rc):
        y = g_ref[c:c + rc, :] * cw[k_width - 1:k_width, :]
        for k in range(k_width - 1):
            y = y + buf[pl.ds(c + SUBLANES - (k_width - 1) + k, rc), :] * cw[k:k + 1, :]
        o_ref[c:c + rc, :] = (_silu(y) * acc[c:c + rc]).astype(o_ref.dtype)


def _ffn_up(a, w_up, gate, conv_w, *, k_width, tm, tn):
    m, k = a.shape
    n = w_up.shape[1]
    n_tiles = pl.cdiv(n, tn)
    pz = 2 if n_tiles % 2 == 0 else 1
    sub = tm // SUBLANES
    col = lambda jp, jj: jp * pz + jj
    tile = lambda jp, i, jj: jp * pz + jnp.where(i == 0, jj, pz - 1)
    return pl.pallas_call(
        functools.partial(_ffn_up_body, k_width),
        grid=(n_tiles // pz, m // tm, pz),
        in_specs=[pl.BlockSpec((tm, k), lambda jp, i, jj: (i, 0)),
                  pl.BlockSpec((k, tn), lambda jp, i, jj: (0, tile(jp, i, jj))),
                  pl.BlockSpec((tm, tn), lambda jp, i, jj: (i, col(jp, jj))),
                  pl.BlockSpec((SUBLANES, tn), lambda jp, i, jj: (jnp.maximum(i * sub - 1, 0), col(jp, jj))),
                  pl.BlockSpec((k_width, tn), lambda jp, i, jj: (0, col(jp, jj)))],
        out_specs=pl.BlockSpec((tm, tn), lambda jp, i, jj: (i, col(jp, jj))),
        out_shape=jax.ShapeDtypeStruct((m, n), BF16),
        scratch_shapes=[pltpu.VMEM((pz, k, tn), BF16), pltpu.VMEM((tm + SUBLANES, tn), F32)],
        compiler_params=_params(("parallel", "arbitrary", "arbitrary")),
        name="ffn_up_conv",
    )(a, w_up, gate, gate, conv_w)


def _conv_body(k_width, zero_first, epilogue, n_extra, x_ref, prev_ref, w_ref, *refs):
    extra = refs[:n_extra]
    o_ref, buf = refs[-2], refs[-1]
    tr = x_ref.shape[0]
    prev = prev_ref[...]
    if zero_first:
        prev = jnp.where(pl.program_id(0) == 0, jnp.zeros_like(prev), prev)
    buf[0:SUBLANES, :] = prev
    buf[SUBLANES:SUBLANES + tr, :] = x_ref[...]
    w = w_ref[...]
    y = x_ref[...] * w[k_width - 1:k_width, :]
    for k in range(k_width - 1):
        y = y + buf[pl.ds(SUBLANES - (k_width - 1) + k, tr), :] * w[k:k + 1, :]
    o_ref[...] = epilogue(y, *[e[...] for e in extra]).astype(o_ref.dtype)


def _conv(x, prev, w, *, k_width, rows, row0, tr, tc, col0, n_cols, zero_first,
          epilogue, extras=(), out_dtype, name, out_rows=None, into=None):
    assert row0 % tr == 0 and rows % tr == 0 and col0 % tc == 0 and tr % SUBLANES == 0
    out_rows = rows if out_rows is None else out_rows
    o_off = 0 if into is None else row0 // tr
    r_off, c_off = row0 // tr, col0 // tc
    sub = tr // SUBLANES
    if prev is None:
        prev_arr = x
        prev_spec = pl.BlockSpec(
            (SUBLANES, tc), lambda i, j: (jnp.maximum((i + r_off) * sub - 1, 0), j + c_off))
    else:
        prev_arr = prev
        prev_spec = pl.BlockSpec((SUBLANES, tc), lambda i, j: (i, j))
    in_specs = [pl.BlockSpec((tr, tc), lambda i, j: (i + r_off, j + c_off)),
                prev_spec,
                pl.BlockSpec((k_width, tc), lambda i, j: (0, j))]
    args = [x, prev_arr, w]
    for arr, ecol0 in extras:
        assert ecol0 % tc == 0
        e_off = 0 if arr.shape[0] == rows else r_off
        in_specs.append(pl.BlockSpec((tr, tc), lambda i, j, c=ecol0 // tc, e=e_off: (i + e, j + c)))
        args.append(arr)
    alias = _into(into, args, in_specs)
    if into is not None:
        assert into.shape[1] == n_cols and into.dtype == out_dtype
        out_rows = into.shape[0]
    return pl.pallas_call(
        functools.partial(_conv_body, k_width, zero_first, epilogue, len(extras)),
        grid=(rows // tr, pl.cdiv(n_cols, tc)),
        in_specs=in_specs,
        out_specs=pl.BlockSpec((tr, tc), lambda i, j: (i + o_off, j)),
        out_shape=jax.ShapeDtypeStruct((out_rows, n_cols), out_dtype),
        input_output_aliases=alias,
        scratch_shapes=[pltpu.VMEM((tr + SUBLANES, tc), F32)],
        compiler_params=_params(("parallel", "parallel")),
        name=name,
    )(*args)


def _dn_act(dn_width, y):
    y = _silu(y)
    if y.shape[1] == 3 * dn_width:
        outs = []
        for h in range(3 * dn_width // HEAD_DIM):
            yh = y[:, h * HEAD_DIM:(h + 1) * HEAD_DIM]
            sec = h * HEAD_DIM // dn_width
            if sec < 2:
                ss = jnp.sum(yh * yh, axis=-1, keepdims=True)
                yh = yh * lax.rsqrt(ss + EPS) * (HEAD_DIM ** -0.5 if sec == 0 else 1.0)
            outs.append(yh)
        return jnp.concatenate(outs, axis=-1)
    sec = pl.program_id(1)
    scale = jnp.where(sec == 0, HEAD_DIM ** -0.5, 1.0).astype(F32)
    outs = []
    for h in range(dn_width // HEAD_DIM):
        yh = y[:, h * HEAD_DIM:(h + 1) * HEAD_DIM]
        ss = jnp.sum(yh * yh, axis=-1, keepdims=True)
        yn = yh * lax.rsqrt(ss + EPS) * scale
        outs.append(jnp.where(sec < 2, yn, yh))
    return jnp.concatenate(outs, axis=-1)


def _ffn_act(y, up):
    return _silu(y) * up


def _gate_body(chunk, ba_ref, alog_ref, dtb_ref, beta_ref, g_ref):
    x = ba_ref[...]
    tr = x.shape[0]
    beta_ref[...] = _sigmoid(x)
    z = x + dtb_ref[...]
    softplus = jnp.maximum(z, 0.0) + jnp.log1p(jnp.exp(-jnp.abs(z)))
    g = -jnp.exp(alog_ref[...]) * softplus
    i = lax.broadcasted_iota(jnp.int32, (tr, tr), 0)
    j = lax.broadcasted_iota(jnp.int32, (tr, tr), 1)
    tri = jnp.where((i // chunk == j // chunk) & (j <= i), 1.0, 0.0).astype(F32)
    g_ref[...] = jnp.dot(tri, g, preferred_element_type=F32, precision=lax.Precision.HIGHEST)


def _gates(ba, a_log, dt_bias, *, rows, row0, chunk, tr):
    n_heads = a_log.shape[0]
    pad = lambda v: jnp.zeros((1, LANES), F32).at[0, n_heads:2 * n_heads].set(v.astype(F32))
    off = row0 // tr
    assert row0 % tr == 0 and rows % tr == 0 and tr % chunk == 0
    return pl.pallas_call(
        functools.partial(_gate_body, chunk),
        grid=(rows // tr,),
        in_specs=[pl.BlockSpec((tr, LANES), lambda i: (i + off, 0)),
                  pl.BlockSpec((1, LANES), lambda i: (0, 0)),
                  pl.BlockSpec((1, LANES), lambda i: (0, 0))],
        out_specs=[pl.BlockSpec((tr, LANES), lambda i: (i, 0)),
                   pl.BlockSpec((tr, LANES), lambda i: (i, 0))],
        out_shape=[jax.ShapeDtypeStruct((rows, LANES), F32)] * 2,
        compiler_params=_params(("parallel",)),
        name="dn_gates",
    )(ba, pad(a_log), pad(dt_bias))


def _split3(x):
    hi = x.astype(BF16)
    return hi, (x - hi.astype(F32)).astype(BF16)


def _dot3(a, b):
    lhs = jnp.concatenate([a[0], a[0], a[1]], axis=1)
    rhs = jnp.concatenate([b[0], b[1], b[0]], axis=0)
    return jnp.dot(lhs, rhs, preferred_element_type=F32)


def _unit_lower_inverse(a_mats, ii, jj, chunk):
    base = min(chunk, 16)
    eye = jnp.where(ii == jj, 1.0, 0.0).astype(F32)
    in_base = ii // base == jj // base
    ds = [jnp.where(in_base, a, 0.0) for a in a_mats]
    ps = [eye - d for d in ds]
    dps = [_split3(d) for d in ds]
    for _ in range(int(math.log2(base)) - 1):
        dps = [_split3(_dot3(dp, dp)) for dp in dps]
        ps = [p + _dot3(_split3(p), dp) for p, dp in zip(ps, dps)]
    size = base
    while size < chunk:
        below = (ii // (2 * size) == jj // (2 * size)) & (ii // size != jj // size)
        es = [_split3(jnp.where(below, a, 0.0)) for a in a_mats]
        pss = [_split3(p) for p in ps]
        ts = [_split3(_dot3(p_s, e)) for p_s, e in zip(pss, es)]
        ps = [p - _dot3(t, p_s) for p, t, p_s in zip(ps, ts, pss)]
        size *= 2
    return ps


def _dn_local_body(chunk, n_heads, hb, q_ref, k_ref, v_ref, beta_ref, g_ref,
                   u_ref, w_ref, qd_ref, kdt_ref, aqk_ref, gl_ref):
    n = q_ref.shape[0]
    hg = pl.program_id(0)
    ii = lax.broadcasted_iota(jnp.int32, (n, n), 0)
    jj = lax.broadcasted_iota(jnp.int32, (n, n), 1)
    same = ii // chunk == jj // chunk
    causal = same & (ii >= jj)
    strict = same & (ii > jj)
    last = jj == (ii // chunk) * chunk + (chunk - 1)
    beta_all = beta_ref[...]
    g_all = g_ref[...]
    lane = lax.broadcasted_iota(jnp.int32, (n, LANES), 1)
    nt = (((1,), (1,)), ((), ()))
    a_mats, rhss = [], []
    for hh in range(hb):
        h = hg * hb + hh
        sl = slice(hh * HEAD_DIM, (hh + 1) * HEAD_DIM)
        q, k, v = q_ref[:, sl], k_ref[:, sl], v_ref[:, sl]
        beta = jnp.sum(jnp.where(lane == h, beta_all, 0.0), axis=-1, keepdims=True)
        gcum = jnp.sum(jnp.where(lane == h + n_heads, g_all, 0.0), axis=-1, keepdims=True)
        g_rows = jnp.broadcast_to(gcum, (n, n))
        g_cols = g_rows.T
        g_last = jnp.sum(jnp.where(last, g_cols, 0.0), axis=-1, keepdims=True)
        gamma = jnp.exp(jnp.where(causal, g_rows - g_cols, -jnp.inf))
        kb = k.astype(BF16)
        kk = lax.dot_general(kb, kb, nt, preferred_element_type=F32)
        qk = lax.dot_general(q.astype(BF16), kb, nt, preferred_element_type=F32)
        a_mats.append(jnp.where(strict, beta * kk * gamma, 0.0))
        eg = jnp.exp(gcum)
        rhss.append(_split3(jnp.concatenate([v * beta, k * (beta * eg)], axis=-1)))
        qd_ref[hh] = q * eg
        kdt_ref[hh] = (k * jnp.exp(g_last - gcum)).T
        aqk_ref[hh] = qk * gamma
        gl_ref[hh] = jnp.broadcast_to(jnp.exp(g_last), (n, LANES))
    invs = _unit_lower_inverse(a_mats, ii, jj, chunk)
    for hh in range(hb):
        sol = _dot3(_split3(invs[hh]), rhss[hh])
        u_ref[hh] = sol[:, :HEAD_DIM]
        w_ref[hh] = sol[:, HEAD_DIM:]


def _dn_local(qkv, beta, gcum, *, rows, row0, chunk, n_heads, dn_width, hb, n):
    assert rows % n == 0 and row0 % n == 0 and n % chunk == 0
    off = row0 // n
    bw = hb * HEAD_DIM
    sec = dn_width // bw
    qkv_spec = lambda s: pl.BlockSpec((n, bw), lambda g, i, s=s: (i + off, g + s * sec))
    head_spec = lambda: pl.BlockSpec((hb, n, HEAD_DIM), lambda g, i: (g, i, 0))
    gate_spec = pl.BlockSpec((n, LANES), lambda g, i: (i, 0))
    shp = lambda *s: jax.ShapeDtypeStruct(s, F32)
    return pl.pallas_call(
        functools.partial(_dn_local_body, chunk, n_heads, hb),
        grid=(n_heads // hb, rows // n),
        in_specs=[qkv_spec(0), qkv_spec(1), qkv_spec(2), gate_spec, gate_spec],
        out_specs=[head_spec(), head_spec(), head_spec(),
                   pl.BlockSpec((hb, HEAD_DIM, n), lambda g, i: (g, 0, i)),
                   pl.BlockSpec((hb, n, n), lambda g, i: (g, i, 0)),
                   head_spec()],
        out_shape=[shp(n_heads, rows, HEAD_DIM), shp(n_heads, rows, HEAD_DIM),
                   shp(n_heads, rows, HEAD_DIM), shp(n_heads, HEAD_DIM, rows),
                   shp(n_heads, rows, n), shp(n_heads, rows, LANES)],
        compiler_params=_params(("parallel", "parallel")),
        name="dn_local",
    )(qkv, qkv, qkv, beta, gcum)


def _gated_head_norm(o, z, gain):
    ms = jnp.mean(o * o, axis=-1, keepdims=True)
    return ((o * lax.rsqrt(ms + EPS)) * gain) * _silu(z)


def _dn_scan_body(chunk, n_heads, u_ref, w_ref, qd_ref, kdt_ref, aqk_ref, gl_ref, z_ref, gain_ref,
                  o_ref, s_out_ref, s_ref):
    step = pl.program_id(0)
    n = u_ref.shape[1]

    @pl.when(step == 0)
    def _():
        s_ref[...] = jnp.zeros_like(s_ref)

    gain = gain_ref[...]
    for c in range(n // chunk):
        rows = slice(c * chunk, (c + 1) * chunk)
        heads = range(n_heads)
        ps = [jnp.dot(jnp.concatenate([w_ref[h, rows, :], qd_ref[h, rows, :]], axis=0).astype(BF16),
                      s_ref[h].astype(BF16), preferred_element_type=F32) for h in heads]
        vbs = [(u_ref[h, rows, :] - ps[h][:chunk]).astype(BF16) for h in heads]
        for h in heads:
            s_ref[h] = s_ref[h] * gl_ref[h, c * chunk:c * chunk + 1, :] + jnp.dot(
                kdt_ref[h, :, rows].astype(BF16), vbs[h], preferred_element_type=F32)
        for h in heads:
            o = ps[h][chunk:] + jnp.dot(aqk_ref[h, rows, rows].astype(BF16), vbs[h],
                                        preferred_element_type=F32)
            cols = slice(h * HEAD_DIM, (h + 1) * HEAD_DIM)
            o_ref[rows, cols] = _gated_head_norm(o, z_ref[rows, cols], gain).astype(o_ref.dtype)

    @pl.when(step == pl.num_programs(0) - 1)
    def _():
        s_out_ref[...] = s_ref[...]


def _dn_scan(u, w, qd, kdt, aqk, gl, zsrc, z_col0, gain, *, chunk, n, out_rows):
    n_heads, rows, _ = u.shape
    dn_width = n_heads * HEAD_DIM
    assert z_col0 % dn_width == 0
    head_spec = lambda: pl.BlockSpec((n_heads, n, HEAD_DIM), lambda i: (0, i, 0))
    return pl.pallas_call(
        functools.partial(_dn_scan_body, chunk, n_heads),
        grid=(rows // n,),
        in_specs=[head_spec(), head_spec(), head_spec(),
                  pl.BlockSpec((n_heads, HEAD_DIM, n), lambda i: (0, 0, i)),
                  pl.BlockSpec((n_heads, n, n), lambda i: (0, i, 0)),
                  head_spec(),
                  pl.BlockSpec((n, dn_width), lambda i: (i, z_col0 // dn_width)),
                  pl.BlockSpec((1, HEAD_DIM), lambda i: (0, 0))],
        out_specs=[pl.BlockSpec((n, dn_width), lambda i: (i, 0)),
                   pl.BlockSpec((n_heads, HEAD_DIM, HEAD_DIM), lambda i: (0, 0, 0))],
        out_shape=[jax.ShapeDtypeStruct((out_rows, dn_width), BF16),
                   jax.ShapeDtypeStruct((n_heads, HEAD_DIM, HEAD_DIM), F32)],
        scratch_shapes=[pltpu.VMEM((n_heads, HEAD_DIM, HEAD_DIM), F32)],
        compiler_params=_params(("arbitrary",)),
        name="dn_scan",
    )(u, w, qd, kdt, aqk, gl, zsrc, gain.reshape(1, HEAD_DIM))


def _dn_sample_body(t_len, u_ref, w_ref, qd_ref, kdt_ref, aqk_ref, gl_ref, z_ref, gain_ref, s_ref,
                    into_ref, o_ref, s_out_ref):
    n = u_ref.shape[1]
    nb = n // t_len
    s = s_ref[0, :, 0]
    w3 = w_ref[0].reshape(nb, t_len, HEAD_DIM)
    q3 = qd_ref[0].reshape(nb, t_len, HEAD_DIM)
    wq = jnp.concatenate([w3, q3], axis=1).astype(BF16)
    p = jnp.einsum('bck,bkd->bcd', wq, s.astype(BF16), preferred_element_type=F32)
    ws = p[:, :t_len].reshape(n, HEAD_DIM)
    qs = p[:, t_len:].reshape(n, HEAD_DIM)
    v_new = u_ref[0] - ws
    vb = v_new.astype(BF16)
    o = qs + jnp.dot(aqk_ref[0].astype(BF16), vb, preferred_element_type=F32)
    kdt = kdt_ref[0]
    ri = lax.broadcasted_iota(jnp.int32, (nb * HEAD_DIM, n), 0)
    ci = lax.broadcasted_iota(jnp.int32, (nb * HEAD_DIM, n), 1)
    zt = jnp.where(ri // HEAD_DIM == ci // t_len, jnp.tile(kdt, (nb, 1)), 0.0).astype(BF16)
    upd = jnp.dot(zt, vb, preferred_element_type=F32).reshape(nb, HEAD_DIM, HEAD_DIM)
    gl = gl_ref[0].reshape(nb, t_len, LANES)[:, 0:1, :]
    s_out_ref[0, :, 0] = s * gl + upd
    o_ref[...] = _gated_head_norm(o, z_ref[...], gain_ref[...]).astype(o_ref.dtype)


def _dn_sample(u, w, qd, kdt, aqk, gl, zsrc, z_row0, z_col0, gain, state, into, *, t_len):
    n_heads, n, _ = u.shape
    nb = n // t_len
    assert z_row0 % n == 0 and z_col0 % HEAD_DIM == 0
    head_spec = lambda: pl.BlockSpec((1, n, HEAD_DIM), lambda h: (h, 0, 0))
    state_spec = pl.BlockSpec((1, nb, 1, HEAD_DIM, HEAD_DIM), lambda h: (0, 0, h, 0, 0))
    in_specs = [head_spec(), head_spec(), head_spec(),
                pl.BlockSpec((1, HEAD_DIM, n), lambda h: (h, 0, 0)),
                pl.BlockSpec((1, n, n), lambda h: (h, 0, 0)),
                head_spec(),
                pl.BlockSpec((n, HEAD_DIM), lambda h: (z_row0 // n, z_col0 // HEAD_DIM + h)),
                pl.BlockSpec((1, HEAD_DIM), lambda h: (0, 0)),
                state_spec]
    args = [u, w, qd, kdt, aqk, gl, zsrc, gain.reshape(1, HEAD_DIM), state]
    alias = _into(into, args, in_specs)
    return pl.pallas_call(
        functools.partial(_dn_sample_body, t_len),
        grid=(n_heads,),
        in_specs=in_specs,
        out_specs=[pl.BlockSpec((n, HEAD_DIM), lambda h: (z_row0 // n, h)), state_spec],
        out_shape=[jax.ShapeDtypeStruct(into.shape, into.dtype),
                   jax.ShapeDtypeStruct(state.shape, F32)],
        input_output_aliases=alias,
        compiler_params=_params(("parallel",)),
        name="dn_sample",
    )(*args)


def _rope_body(n_heads, dil, q_ref, k_ref, v_ref, cos_ref, sin_ref, qo_ref, kvo_ref, slab):
    width = n_heads * HEAD_DIM
    per = q_ref.shape[0] // dil
    cos, sin = cos_ref[...], sin_ref[...]

    def regroup(x, dst_ref, dst_sl):
        if dil == 1:
            dst_ref[0, :, dst_sl] = x.astype(dst_ref.dtype)
            return
        slab[...] = x

        def one_residue(r, carry):
            dst_ref[r, :, dst_sl] = slab[pl.ds(r, per, stride=dil), :].astype(dst_ref.dtype)
            return carry

        lax.fori_loop(0, dil, one_residue, 0)

    for h in range(n_heads):
        sl = slice(h * HEAD_DIM, (h + 1) * HEAD_DIM)
        q = q_ref[:, sl]
        k = k_ref[:, sl]
        regroup(q * cos + pltpu.roll(q, HEAD_DIM // 2, 1) * sin, qo_ref, sl)
        regroup(k * cos + pltpu.roll(k, HEAD_DIM // 2, 1) * sin, kvo_ref, sl)
        regroup(v_ref[:, sl], kvo_ref, slice(width + h * HEAD_DIM, width + (h + 1) * HEAD_DIM))


def _rope(att_src, col0, cos, sin, *, rows, row0, n_heads, tr, dil, q_dtype):
    width = n_heads * HEAD_DIM
    assert row0 % tr == 0 and rows % tr == 0 and col0 % width == 0 and tr % (16 * dil) == 0
    r_off, c_off = row0 // tr, col0 // width
    per = tr // dil
    src = lambda s: pl.BlockSpec((tr, width), lambda i, s=s: (i + r_off, c_off + s))
    tab = pl.BlockSpec((tr, HEAD_DIM), lambda i: (i, 0))
    return pl.pallas_call(
        functools.partial(_rope_body, n_heads, dil),
        grid=(rows // tr,),
        in_specs=[src(0), src(1), src(2), tab, tab],
        out_specs=[pl.BlockSpec((dil, per, width), lambda i: (0, i, 0)),
                   pl.BlockSpec((dil, per, 2 * width), lambda i: (0, i, 0))],
        out_shape=[jax.ShapeDtypeStruct((dil, rows // dil, width), q_dtype),
                   jax.ShapeDtypeStruct((dil, rows // dil, 2 * width), F32)],
        scratch_shapes=[pltpu.VMEM((tr, HEAD_DIM), F32)],
        compiler_params=_params(("parallel",)),
        name="rope",
    )(att_src, att_src, att_src, cos, sin)


def _rope_tables(pos):
    half = HEAD_DIM // 2
    inv_freq = ROPE_THETA ** (-jnp.arange(half, dtype=F32) / half)
    ang = pos.astype(F32)[:, None] * inv_freq[None, :]
    cos, sin = jnp.cos(ang), jnp.sin(ang)
    return jnp.concatenate([cos, cos], axis=-1), jnp.concatenate([-sin, sin], axis=-1)


def _attn_prompt_body(n_heads, dil, q_ref, kc_ref, kp_ref, vc_ref, vp_ref, o_ref, lse_ref):
    nb = ATT_BLOCK
    blk = pl.program_id(0)
    r = pl.program_id(1)
    rows = pl.ds(r, nb, stride=dil) if dil > 1 else slice(None)
    i = lax.broadcasted_iota(jnp.int32, (nb, 2 * nb), 0)
    j = lax.broadcasted_iota(jnp.int32, (nb, 2 * nb), 1)
    mask = (j >= i) & (j <= i + nb) & ((blk > 0) | (j >= nb))
    lane = lax.broadcasted_iota(jnp.int32, (nb, LANES), 1)
    scale = HEAD_DIM ** -0.5
    lse_all = jnp.zeros((nb, LANES), F32)
    for h in range(n_heads):
        sl = slice(h * HEAD_DIM, (h + 1) * HEAD_DIM)
        k = jnp.concatenate([kp_ref[:, sl], kc_ref[:, sl]], axis=0).astype(BF16)
        v = jnp.concatenate([vp_ref[:, sl], vc_ref[:, sl]], axis=0).astype(BF16)
        s = lax.dot_general(q_ref[:, sl], k, (((1,), (1,)), ((), ())), preferred_element_type=F32) * scale
        s = jnp.where(mask, s, -jnp.inf)
        m = jnp.max(s, axis=-1, keepdims=True)
        p = jnp.exp(s - m)
        l = jnp.sum(p, axis=-1, keepdims=True)
        o_ref[h, rows, :] = jnp.dot(p.astype(BF16), v, preferred_element_type=F32) / l
        lse_all = jnp.where(lane == h, m + jnp.log(l), lse_all)
    lse_ref[rows, :] = lse_all


def _attn_prompt(q, kv, dil, n_heads):
    _, m_len, width = q.shape
    nb = ATT_BLOCK
    t_len = m_len * dil
    assert m_len % nb == 0
    src = lambda f: pl.BlockSpec((None, nb, width), f)
    prev = lambda n: jnp.maximum(n - 1, 0)
    return pl.pallas_call(
        functools.partial(_attn_prompt_body, n_heads, dil),
        grid=(m_len // nb, dil),
        in_specs=[src(lambda n, r: (r, n, 0)),
                  src(lambda n, r: (r, n, 0)), src(lambda n, r: (r, prev(n), 0)),
                  src(lambda n, r: (r, n, 1)), src(lambda n, r: (r, prev(n), 1))],
        out_specs=[pl.BlockSpec((n_heads, nb * dil, HEAD_DIM), lambda n, r: (0, n, 0)),
                   pl.BlockSpec((nb * dil, LANES), lambda n, r: (n, 0))],
        out_shape=[jax.ShapeDtypeStruct((n_heads, t_len, HEAD_DIM), F32),
                   jax.ShapeDtypeStruct((t_len, LANES), F32)],
        compiler_params=_params(("parallel", "arbitrary")),
        name="attn_prompt",
    )(q, kv, kv, kv, kv)


def _attn_sample_body(n_heads, dil, stride, q_ref, kvn_ref, cache_ref, o_ref, lse_ref):
    t_len = q_ref.shape[1]
    width = n_heads * HEAD_DIM
    n_buf = math.prod(cache_ref.shape[1:-3])
    n_pair = n_buf * n_heads
    lg_h, lg_t = int(math.log2(n_heads)), int(math.log2(t_len))
    assert (1 << lg_h) == n_heads and (1 << lg_t) == t_len and dil & (dil - 1) == 0

    def cache_part(kv):
        if stride > 1:
            x = cache_ref[0, :, :, kv]
        else:
            x = cache_ref[0, :, kv]
        return x.reshape(n_pair, HEAD_DIM).astype(BF16)

    def heads_on_rows(ref, col0):
        return jnp.concatenate(
            [ref[0, :, col0 + h * HEAD_DIM:col0 + (h + 1) * HEAD_DIM] for h in range(n_heads)],
            axis=0).astype(BF16)

    q_all = heads_on_rows(q_ref, 0)
    k_new = heads_on_rows(kvn_ref, 0)
    v_new = heads_on_rows(kvn_ref, width)
    rq = lax.broadcasted_iota(jnp.int32, (n_heads * t_len, n_pair), 0)
    cc = lax.broadcasted_iota(jnp.int32, (n_heads * t_len, n_pair), 1)
    jq, hq = rq & (t_len - 1), rq >> lg_t
    row, hc = cc >> lg_h, cc & (n_heads - 1)
    if stride > 1:
        idx = (row >> lg_t) * stride + (row & (t_len - 1))
    else:
        idx = row
    mask_buf = (hq == hc) & (idx >= jq) & (((idx - jq) & (dil - 1)) == 0)
    rn = lax.broadcasted_iota(jnp.int32, (n_heads * t_len, n_heads * t_len), 0)
    cn = lax.broadcasted_iota(jnp.int32, (n_heads * t_len, n_heads * t_len), 1)
    jn, jc = rn & (t_len - 1), cn & (t_len - 1)
    mask_new = ((rn >> lg_t) == (cn >> lg_t)) & (jc <= jn) & (((jn - jc) & (dil - 1)) == 0)
    scale = HEAD_DIM ** -0.5
    nt = (((1,), (1,)), ((), ()))
    s_buf = lax.dot_general(q_all, cache_part(0), nt, preferred_element_type=F32) * scale
    s_new = lax.dot_general(q_all, k_new, nt, preferred_element_type=F32) * scale
    s_buf = jnp.where(mask_buf, s_buf, -jnp.inf)
    s_new = jnp.where(mask_new, s_new, -jnp.inf)
    m = jnp.maximum(jnp.max(s_buf, axis=-1, keepdims=True), jnp.max(s_new, axis=-1, keepdims=True))
    p_buf = jnp.exp(s_buf - m)
    p_new = jnp.exp(s_new - m)
    l = jnp.sum(p_buf, axis=-1, keepdims=True) + jnp.sum(p_new, axis=-1, keepdims=True)
    o = (jnp.dot(p_buf.astype(BF16), cache_part(1), preferred_element_type=F32)
         + jnp.dot(p_new.astype(BF16), v_new, preferred_element_type=F32)) / l
    lse = m + jnp.log(l)
    lane = lax.broadcasted_iota(jnp.int32, (t_len, LANES), 1)
    lse_all = jnp.zeros((t_len, LANES), F32)
    for h in range(n_heads):
        o_ref[h] = o[h * t_len:(h + 1) * t_len]
        lse_all = jnp.where(lane == h, lse[h * t_len:(h + 1) * t_len], lse_all)
    lse_ref[...] = lse_all


def _attn_sample(q, kvn, cache, win, dil, n_heads, t_len):
    bsz, buf_len = cache.shape[:2]
    width = n_heads * HEAD_DIM
    assert buf_len == win and win == ATT_BLOCK * dil, "window buffer must be full"
    if dil >= 2 * t_len:
        stride = dil
        cache_v = cache.reshape(bsz, buf_len // dil, dil, 2, n_heads, HEAD_DIM)
        cache_spec = pl.BlockSpec((1, buf_len // dil, t_len, 2, n_heads, HEAD_DIM),
                                  lambda b: (b, 0, 0, 0, 0, 0))
    else:
        stride = 1
        cache_v = cache
        cache_spec = pl.BlockSpec((1, buf_len, 2, n_heads, HEAD_DIM), lambda b: (b, 0, 0, 0, 0))
    return pl.pallas_call(
        functools.partial(_attn_sample_body, n_heads, dil, stride),
        grid=(bsz,),
        in_specs=[pl.BlockSpec((1, t_len, width), lambda b: (0, b, 0)),
                  pl.BlockSpec((1, t_len, 2 * width), lambda b: (0, b, 0)),
                  cache_spec],
        out_specs=[pl.BlockSpec((n_heads, t_len, HEAD_DIM), lambda b: (0, b, 0)),
                   pl.BlockSpec((t_len, LANES), lambda b: (b, 0))],
        out_shape=[jax.ShapeDtypeStruct((n_heads, bsz * t_len, HEAD_DIM), F32),
                   jax.ShapeDtypeStruct((bsz * t_len, LANES), F32)],
        compiler_params=_params(("parallel",)),
        name="attn_sample",
    )(q, kvn, cache_v)


def _merge_body(n_heads, o0, o1, o2, l0, l1, l2, *refs):
    out_ref = refs[-1]
    la, lb, lc = l0[...], l1[...], l2[...]
    m = jnp.maximum(jnp.maximum(la, lb), lc)
    ea, eb, ec = jnp.exp(la - m), jnp.exp(lb - m), jnp.exp(lc - m)
    tot = ea + eb + ec
    wa, wb, wc = ea / tot, eb / tot, ec / tot
    for h in range(n_heads):
        sl = slice(h * HEAD_DIM, (h + 1) * HEAD_DIM)
        col = slice(h, h + 1)
        out_ref[:, sl] = (wa[:, col] * o0[h] + wb[:, col] * o1[h]
                          + wc[:, col] * o2[h]).astype(out_ref.dtype)


def _merge(outs, lses, tr, *, out_rows=None, row0=0, into=None):
    n_heads, rows, _ = outs[0].shape
    width = n_heads * HEAD_DIM
    assert row0 % tr == 0 and rows % tr == 0
    off = row0 // tr
    spec = pl.BlockSpec((n_heads, tr, HEAD_DIM), lambda i: (0, i, 0))
    lspec = pl.BlockSpec((tr, LANES), lambda i: (i, 0))
    in_specs = [spec] * 3 + [lspec] * 3
    args = [*outs, *lses]
    alias = _into(into, args, in_specs)
    out_rows = into.shape[0] if into is not None else (rows if out_rows is None else out_rows)
    return pl.pallas_call(
        functools.partial(_merge_body, n_heads),
        grid=(rows // tr,),
        in_specs=in_specs,
        out_specs=pl.BlockSpec((tr, width), lambda i: (i + off, 0)),
        out_shape=jax.ShapeDtypeStruct((out_rows, width), BF16),
        input_output_aliases=alias,
        compiler_params=_params(("parallel",)),
        name="merge_groups",
    )(*args)


def _residual_rows(n_head, xp, xs):
    last = pl.program_id(1) == pl.num_programs(1) - 1
    return jnp.where(last, jnp.concatenate([xp[:n_head], xs], axis=0), xp)


def _layer(x_p, x_s, bsz, t_s, past_len, caches, dn_state, dn_conv_state, ffn_conv_state,
           norm_mix, w_in, dn_conv_w, dn_a_log, dn_dt_bias, dn_out_norm, w_branch_dn,
           w_branch_att, w_out, norm_ffn, w_ffn_gate, w_ffn_up, ffn_conv_w, w_ffn_down):
    t_p, d_model = x_p.shape
    n_s = bsz * t_s
    r_all = t_p + n_s
    n_heads_dn = dn_a_log.shape[0]
    dn_width = n_heads_dn * HEAD_DIM
    n_groups = len(ATT_GROUPS)
    att_width = w_branch_att.shape[0]
    n_heads_att = att_width // HEAD_DIM
    d_ff = w_ffn_gate.shape[1]
    tm = _pick_tile(r_all, 1100, 16)

    c_ba = 4 * dn_width
    c_att = c_ba + 2 * n_heads_dn
    n1 = _rmsnorm(x_p, norm_mix, BF16, out_rows=r_all)
    n1 = _rmsnorm(x_s, norm_mix, BF16, out_row0=t_p, into=n1)
    w_in_t = w_in.T
    proj = functools.partial(_matmul_w, n1, w_in_t, transposed=True, out_dtype=F32, tm=tm)
    qkvz = proj(col0=0, n=c_ba, tn=512, name="proj_dn")
    ba = proj(col0=c_ba, n=LANES, tn=LANES, name="proj_ba")
    rest = proj(col0=c_att, tn=512, name="proj_att")
    col_gdn = 3 * n_groups * att_width
    col_gatt = col_gdn + d_model

    dn_act = functools.partial(_dn_act, dn_width)
    conv_args = dict(k_width=DN_CONV, tc=dn_width, col0=0, n_cols=3 * dn_width,
                     epilogue=dn_act, out_dtype=F32)
    qkv_p = _conv(qkvz, None, dn_conv_w, rows=t_p, row0=0, tr=256, zero_first=True,
                  name="dn_conv_prompt", **conv_args)
    state_pad = jnp.pad(dn_conv_state, ((0, 0), (SUBLANES - (DN_CONV - 1), 0), (0, 0)))
    conv_args["tc"] = 3 * dn_width
    qkv_s = _conv(qkvz, state_pad.reshape(bsz * SUBLANES, -1), dn_conv_w, rows=n_s, row0=t_p,
                  tr=t_s, zero_first=False, name="dn_conv_sample", **conv_args)
    p_dn_conv = qkvz[t_p - (DN_CONV - 1):t_p, :3 * dn_width][None]
    s_dn_conv = qkvz[t_p:].reshape(bsz, t_s, -1)[:, t_s - (DN_CONV - 1):, :3 * dn_width]

    beta_p, g_p = _gates(ba, dn_a_log, dn_dt_bias, rows=t_p, row0=0, chunk=DN_CHUNK, tr=512)
    beta_s, g_s = _gates(ba, dn_a_log, dn_dt_bias, rows=n_s, row0=t_p, chunk=t_s, tr=n_s)

    loc_p = _dn_local(qkv_p, beta_p, g_p, rows=t_p, row0=0, chunk=DN_CHUNK,
                      n_heads=n_heads_dn, dn_width=dn_width, hb=8, n=128)
    o_dn, p_dn = _dn_scan(*loc_p, qkvz, 3 * dn_width, dn_out_norm, chunk=DN_CHUNK, n=128,
                          out_rows=r_all)
    loc_s = _dn_local(qkv_s, beta_s, g_s, rows=n_s, row0=0, chunk=t_s,
                      n_heads=n_heads_dn, dn_width=dn_width, hb=4, n=n_s)
    o_dn, s_dn = _dn_sample(*loc_s, qkvz, t_p, 3 * dn_width, dn_out_norm, dn_state, o_dn, t_len=t_s)

    cos_p, sin_p = _rope_tables(jnp.arange(t_p, dtype=jnp.int32))
    cos_s, sin_s = _rope_tables(past_len + jnp.tile(jnp.arange(t_s, dtype=jnp.int32), bsz))
    outs_p, lses_p, outs_s, lses_s, p_kv, s_kv = [], [], [], [], [], []
    for gi, (win, dil) in enumerate(ATT_GROUPS):
        col = 3 * gi * att_width
        q_p, kv_p = _rope(rest, col, cos_p, sin_p, rows=t_p, row0=0, n_heads=n_heads_att,
                          tr=512, dil=dil, q_dtype=BF16)
        q_s, kv_s = _rope(rest, col, cos_s, sin_s, rows=n_s, row0=t_p, n_heads=n_heads_att,
                          tr=n_s, dil=1, q_dtype=F32)
        o, lse = _attn_prompt(q_p, kv_p, dil, n_heads_att)
        outs_p.append(o)
        lses_p.append(lse)
        o, lse = _attn_sample(q_s, kv_s, caches[gi], win, dil, n_heads_att, t_s)
        outs_s.append(o)
        lses_s.append(lse)
        keep = min(win, t_p)
        tail = kv_p[:, (t_p - keep) // dil:].transpose(1, 0, 2)
        p_kv.append(tail.reshape(1, keep, 2, n_heads_att, HEAD_DIM))
        s_kv.append(kv_s.reshape(bsz, t_s, 2, n_heads_att, HEAD_DIM))
    o_att = _merge(outs_p, lses_p, 256, out_rows=r_all)
    o_att = _merge(outs_s, lses_s, n_s, row0=t_p, into=o_att)

    y_dn = _matmul_w(o_dn, w_branch_dn, out_dtype=F32, tm=tm, tn=512,
                     epilogue=lambda acc, g: _sigmoid(g) * acc, extras=[(rest, col_gdn)],
                     name="branch_dn")
    mix = _matmul_w(o_att, w_branch_att, out_dtype=BF16, tm=tm, tn=512,
                    epilogue=lambda acc, g, y: y + _sigmoid(g) * acc,
                    extras=[(rest, col_gatt), (y_dn, 0)], name="branch_att")
    n_head = t_p - (r_all - tm)
    assert 0 <= n_head and n_head % SUBLANES == 0 and n_s < tm, "sample rows must share the last row tile"
    x1 = _matmul_w(mix, w_out, out_dtype=F32, tm=tm, tn=512,
                   epilogue=lambda acc, xp, xs: _residual_rows(n_head, xp, xs) + acc,
                   extras=[(x_p, 0), (x_s, 0)], name="out_proj")

    n2 = _rmsnorm(x1, norm_ffn, BF16)
    gate = _matmul_w(n2, w_ffn_gate, out_dtype=F32, tm=tm, tn=512, name="ffn_gate")
    h = _ffn_up(n2, w_ffn_up, gate, ffn_conv_w, k_width=FFN_CONV, tm=tm, tn=512)
    up_s = _matmul_w(n2, w_ffn_up, out_dtype=F32, tm=n_s, tn=512, row0=t_p, rows=n_s, name="ffn_up_sample")
    fstate_pad = jnp.pad(ffn_conv_state, ((0, 0), (SUBLANES - (FFN_CONV - 1), 0), (0, 0)))
    h = _conv(gate, fstate_pad.reshape(bsz * SUBLANES, -1), ffn_conv_w, k_width=FFN_CONV,
              rows=n_s, row0=t_p, tr=t_s, tc=d_ff, col0=0, n_cols=d_ff, zero_first=False,
              epilogue=_ffn_act, extras=[(up_s, 0)], out_dtype=BF16, name="ffn_conv_sample", into=h)
    p_ffn_conv = gate[t_p - (FFN_CONV - 1):t_p][None]
    s_ffn_conv = gate[t_p:].reshape(bsz, t_s, d_ff)[:, t_s - (FFN_CONV - 1):]
    tm_down = _pick_tile(r_all, 600, 16)
    x2 = _matmul(h, w_ffn_down.astype(BF16), out_dtype=F32, tm=tm_down, tn=256,
                 epilogue=lambda acc, r: r + acc, extras=[(x1, 0)], name="ffn_down")
    states_p = (p_kv[0], p_kv[1], p_kv[2], p_dn[None], p_dn_conv, p_ffn_conv)
    states_s = (s_kv[0], s_kv[1], s_kv[2], s_dn[0], s_dn_conv, s_ffn_conv)
    return x2, states_p, states_s


def kernel(x_prompt, x_sample, cache_kv_w128, cache_kv_w512, cache_kv_w2048, state_dn, state_dn_conv, state_ffn_conv, norm_mix, w_in, dn_conv_w, dn_a_log, dn_dt_bias, dn_out_norm, w_branch_dn, w_branch_att, w_out, norm_ffn, w_ffn_gate, w_ffn_up, ffn_conv_w, w_ffn_down, norm_final):
    b_p, t_p, d_model = x_prompt.shape
    bsz, t_s, _ = x_sample.shape
    depth = w_in.shape[0]
    assert b_p == 1 and depth == 1, "one prompt sequence, one layer"
    l = 0
    x, st_p, st_s = _layer(
        x_prompt.reshape(t_p, d_model), x_sample.reshape(bsz * t_s, d_model), bsz, t_s, PAST_LEN,
        (cache_kv_w128[l], cache_kv_w512[l], cache_kv_w2048[l]),
        state_dn[l:l + 1], state_dn_conv[l], state_ffn_conv[l],
        norm_mix[l], w_in[l], dn_conv_w[l], dn_a_log[l], dn_dt_bias[l], dn_out_norm[l],
        w_branch_dn[l], w_branch_att[l], w_out[l], norm_ffn[l], w_ffn_gate[l], w_ffn_up[l],
        ffn_conv_w[l], w_ffn_down[l])
    y_prompt = _rmsnorm(x, norm_final, F32, row0=0, rows=t_p).reshape(1, t_p, d_model)
    y_sample = _rmsnorm(x, norm_final, F32, row0=t_p, rows=bsz * t_s).reshape(bsz, t_s, d_model)
    return (y_prompt, y_sample) + tuple(s[None] for s in st_p) + tuple(s[None] for s in st_s)
```

```python
import functools
import math

import jax
import jax.numpy as jnp
from jax import lax
from jax.experimental import pallas as pl
from jax.experimental.pallas import tpu as pltpu

F32 = jnp.float32
BF16 = jnp.bfloat16

EPS = 1e-6
ROPE_THETA = 10000.0
HEAD_DIM = 128
DN_CHUNK = 64
DN_CONV = 4
FFN_CONV = 3
ATT_GROUPS = ((128, 1), (512, 4), (2048, 16))
ATT_BLOCK = 128
PAST_LEN = 8192
SUBLANES = 8
LANES = 128
VMEM_LIMIT = 60 * 1024 * 1024


def _pick_tile(n, target, mult):
    best = None
    for t in range(mult, min(n, target) + 1, mult):
        if n % t == 0:
            best = t
    assert best is not None, (n, target, mult)
    return best


def _params(sem):
    return pltpu.CompilerParams(dimension_semantics=sem, vmem_limit_bytes=VMEM_LIMIT)


def _sigmoid(x):
    return 1.0 / (1.0 + jnp.exp(-x))


def _silu(x):
    return x * _sigmoid(x)


def _into(into, args, in_specs):
    if into is None:
        return {}
    in_specs.append(pl.BlockSpec(memory_space=pl.ANY))
    args.append(into)
    return {len(args) - 1: 0}


def _rmsnorm_body(x_ref, g_ref, *refs):
    o_ref = refs[-1]
    x = x_ref[...]
    ms = jnp.mean(x * x, axis=-1, keepdims=True)
    o_ref[...] = ((x * lax.rsqrt(ms + EPS)) * g_ref[...]).astype(o_ref.dtype)


def _rmsnorm(x, gain, out_dtype, *, row0=0, rows=None, tile=512, out_rows=None, out_row0=0, into=None):
    d = x.shape[1]
    rows = x.shape[0] if rows is None else rows
    tr = _pick_tile(math.gcd(math.gcd(rows, row0), out_row0), tile, 16)
    off, o_off = row0 // tr, out_row0 // tr
    in_specs = [pl.BlockSpec((tr, d), lambda i: (i + off, 0)),
                pl.BlockSpec((1, d), lambda i: (0, 0))]
    args = [x, gain.reshape(1, d)]
    alias = _into(into, args, in_specs)
    out_rows = into.shape[0] if into is not None else (rows if out_rows is None else out_rows)
    return pl.pallas_call(
        _rmsnorm_body,
        grid=(rows // tr,),
        in_specs=in_specs,
        out_specs=pl.BlockSpec((tr, d), lambda i: (i + o_off, 0)),
        out_shape=jax.ShapeDtypeStruct((out_rows, d), out_dtype),
        input_output_aliases=alias,
        compiler_params=_params(("parallel",)),
        name="rmsnorm",
    )(*args)


def _mm_body(epilogue, n_extra, a_ref, b_ref, *refs):
    extra = refs[:n_extra]
    o_ref = refs[n_extra]
    acc = jnp.dot(a_ref[...], b_ref[...], preferred_element_type=F32)
    o_ref[...] = epilogue(acc, *[e[...] for e in extra]).astype(o_ref.dtype)


def _matmul(a, b, *, out_dtype, tm, tn, epilogue=None, extras=(), name="matmul"):
    m, k = a.shape
    n = b.shape[1]
    if epilogue is None:
        epilogue = lambda acc: acc
    in_specs = [pl.BlockSpec((tm, k), lambda i, j: (i, 0)),
                pl.BlockSpec((k, tn), lambda i, j: (0, j))]
    args = [a, b]
    for arr, col0 in extras:
        assert col0 % tn == 0
        in_specs.append(pl.BlockSpec((tm, tn), lambda i, j, c=col0 // tn: (i, j + c)))
        args.append(arr)
    return pl.pallas_call(
        functools.partial(_mm_body, epilogue, len(extras)),
        grid=(m // tm, pl.cdiv(n, tn)),
        in_specs=in_specs,
        out_specs=pl.BlockSpec((tm, tn), lambda i, j: (i, j)),
        out_shape=jax.ShapeDtypeStruct((m, n), out_dtype),
        compiler_params=_params(("parallel", "parallel")),
        name=name,
    )(*args)


def _mmw_body(epilogue, n_extra, shift, transposed, a_ref, b_ref, *refs):
    if shift:
        bn_ref, refs = refs[0], refs[1:]
    extra = refs[:n_extra]
    o_ref, w_scr = refs[n_extra], refs[n_extra + 1]
    slot = pl.program_id(2)

    @pl.when(pl.program_id(1) == 0)
    def _():
        n_rows = b_ref.shape[0]
        step = min(n_rows, 256)
        for c in range(0, n_rows, step):
            rows = slice(c, c + step)
            if not shift:
                w = b_ref[rows, :]
            elif not transposed:
                w = jnp.concatenate([b_ref[rows, shift:], bn_ref[rows, :shift]], axis=1)
            elif c + step < n_rows:
                w = b_ref[c + shift:c + step + shift, :]
            else:
                w = jnp.concatenate([b_ref[c + shift:, :], bn_ref[...]], axis=0)
            w_scr[slot, rows, :] = w.astype(BF16)

    dims = (((1,), (1,)), ((), ())) if transposed else (((1,), (0,)), ((), ()))
    acc = lax.dot_general(a_ref[...], w_scr[slot], dims, preferred_element_type=F32)
    o_ref[...] = epilogue(acc, *[e[...] for e in extra]).astype(o_ref.dtype)


def _matmul_w(a, b, *, col0=0, n=None, transposed=False, out_dtype, tm, tn, epilogue=None,
              extras=(), name="matmul_w", row0=0, rows=None):
    k = a.shape[1]
    m = a.shape[0] if rows is None else rows
    assert row0 % tm == 0 and m % tm == 0
    r_off = row0 // tm
    n_total = b.shape[0] if transposed else b.shape[1]
    n = n_total - col0 if n is None else n
    shift = col0 % tn if transposed else col0 % LANES
    assert (col0 - shift) % tn == 0 and tn % LANES == 0
    if transposed and shift:
        assert tn % shift == 0 and shift % SUBLANES == 0
    jb = (col0 - shift) // tn
    n_tiles = pl.cdiv(n, tn)
    pz = 2 if n_tiles % 2 == 0 else 1
    if epilogue is None:
        epilogue = lambda acc: acc

    def tile(jp, i, jj):
        return jp * pz + jnp.where(i == 0, jj, pz - 1) + jb

    if transposed:
        b_spec = pl.BlockSpec((tn, k), lambda jp, i, jj: (tile(jp, i, jj), 0))
        bn_spec = pl.BlockSpec((shift or SUBLANES, k),
                               lambda jp, i, jj: ((tile(jp, i, jj) + 1) * (tn // (shift or tn)), 0))
        scr = pltpu.VMEM((pz, tn, k), BF16)
    else:
        b_spec = pl.BlockSpec((k, tn), lambda jp, i, jj: (0, tile(jp, i, jj)))
        bn_spec = pl.BlockSpec((k, LANES), lambda jp, i, jj: (0, (tile(jp, i, jj) + 1) * (tn // LANES)))
        scr = pltpu.VMEM((pz, k, tn), BF16)
    in_specs = [pl.BlockSpec((tm, k), lambda jp, i, jj: (i + r_off, 0)), b_spec]
    args = [a, b]
    if shift:
        in_specs.append(bn_spec)
        args.append(b)
    for arr, ecol0 in extras:
        assert ecol0 % tn == 0
        e_rows = min(tm, arr.shape[0])
        in_specs.append(pl.BlockSpec(
            (e_rows, tn), lambda jp, i, jj, c=ecol0 // tn, s=int(e_rows == tm): (i * s, jp * pz + jj + c)))
        args.append(arr)
    return pl.pallas_call(
        functools.partial(_mmw_body, epilogue, len(extras), shift, transposed),
        grid=(n_tiles // pz, m // tm, pz),
        in_specs=in_specs,
        out_specs=pl.BlockSpec((tm, tn), lambda jp, i, jj: (i, jp * pz + jj)),
        out_shape=jax.ShapeDtypeStruct((m, n), out_dtype),
        scratch_shapes=[scr],
        compiler_params=_params(("parallel", "arbitrary", "arbitrary")),
        name=name,
    )(*args)


def _ffn_up_body(k_width, a_ref, b_ref, g_ref, gprev_ref, cw_ref, o_ref, w_scr, buf):
    i, slot = pl.program_id(1), pl.program_id(2)

    @pl.when(i == 0)
    def _():
        step = min(b_ref.shape[0], 256)
        for c in range(0, b_ref.shape[0], step):
            w_scr[slot, c:c + step, :] = b_ref[c:c + step, :].astype(BF16)

    tm = a_ref.shape[0]
    acc = jnp.dot(a_ref[...], w_scr[slot], preferred_element_type=F32)
    prev = gprev_ref[...]
    buf[0:SUBLANES, :] = jnp.where(i == 0, jnp.zeros_like(prev), prev)
    buf[SUBLANES:SUBLANES + tm, :] = g_ref[...]
    cw = cw_ref[...]
    rc = _pick_tile(tm, 128, 16)
    for c in range(0, tm, rc):
        y = g_ref[c:c + rc, :] * cw[k_width - 1:k_width, :]
        for k in range(k_width - 1):
            y = y + buf[pl.ds(c + SUBLANES - (k_width - 1) + k, rc), :] * cw[k:k + 1, :]
        o_ref[c:c + rc, :] = (_silu(y) * acc[c:c + rc]).astype(o_ref.dtype)


def _ffn_up(a, w_up, gate, conv_w, *, k_width, tm, tn):
    m, k = a.shape
    n = w_up.shape[1]
    n_tiles = pl.cdiv(n, tn)
    pz = 2 if n_tiles % 2 == 0 else 1
    sub = tm // SUBLANES
    col = lambda jp, jj: jp * pz + jj
    tile = lambda jp, i, jj: jp * pz + jnp.where(i == 0, jj, pz - 1)
    return pl.pallas_call(
        functools.partial(_ffn_up_body, k_width),
        grid=(n_tiles // pz, m // tm, pz),
        in_specs=[pl.BlockSpec((tm, k), lambda jp, i, jj: (i, 0)),
                  pl.BlockSpec((k, tn), lambda jp, i, jj: (0, tile(jp, i, jj))),
                  pl.BlockSpec((tm, tn), lambda jp, i, jj: (i, col(jp, jj))),
                  pl.BlockSpec((SUBLANES, tn), lambda jp, i, jj: (jnp.maximum(i * sub - 1, 0), col(jp, jj))),
                  pl.BlockSpec((k_width, tn), lambda jp, i, jj: (0, col(jp, jj)))],
        out_specs=pl.BlockSpec((tm, tn), lambda jp, i, jj: (i, col(jp, jj))),
        out_shape=jax.ShapeDtypeStruct((m, n), BF16),
        scratch_shapes=[pltpu.VMEM((pz, k, tn), BF16), pltpu.VMEM((tm + SUBLANES, tn), F32)],
        compiler_params=_params(("parallel", "arbitrary", "arbitrary")),
        name="ffn_up_conv",
    )(a, w_up, gate, gate, conv_w)


def _conv_body(k_width, zero_first, epilogue, n_extra, x_ref, prev_ref, w_ref, *refs):
    extra = refs[:n_extra]
    o_ref, buf = refs[-2], refs[-1]
    tr = x_ref.shape[0]
    prev = prev_ref[...]
    if zero_first:
        prev = jnp.where(pl.program_id(0) == 0, jnp.zeros_like(prev), prev)
    buf[0:SUBLANES, :] = prev
    buf[SUBLANES:SUBLANES + tr, :] = x_ref[...]
    w = w_ref[...]
    y = x_ref[...] * w[k_width - 1:k_width, :]
    for k in range(k_width - 1):
        y = y + buf[pl.ds(SUBLANES - (k_width - 1) + k, tr), :] * w[k:k + 1, :]
    o_ref[...] = epilogue(y, *[e[...] for e in extra]).astype(o_ref.dtype)


def _conv(x, prev, w, *, k_width, rows, row0, tr, tc, col0, n_cols, zero_first,
          epilogue, extras=(), out_dtype, name, out_rows=None, into=None):
    assert row0 % tr == 0 and rows % tr == 0 and col0 % tc == 0 and tr % SUBLANES == 0
    out_rows = rows if out_rows is None else out_rows
    o_off = 0 if into is None else row0 // tr
    r_off, c_off = row0 // tr, col0 // tc
    sub = tr // SUBLANES
    if prev is None:
        prev_arr = x
        prev_spec = pl.BlockSpec(
            (SUBLANES, tc), lambda i, j: (jnp.maximum((i + r_off) * sub - 1, 0), j + c_off))
    else:
        prev_arr = prev
        prev_spec = pl.BlockSpec((SUBLANES, tc), lambda i, j: (i, j))
    in_specs = [pl.BlockSpec((tr, tc), lambda i, j: (i + r_off, j + c_off)),
                prev_spec,
                pl.BlockSpec((k_width, tc), lambda i, j: (0, j))]
    args = [x, prev_arr, w]
    for arr, ecol0 in extras:
        assert ecol0 % tc == 0
        e_off = 0 if arr.shape[0] == rows else r_off
        in_specs.append(pl.BlockSpec((tr, tc), lambda i, j, c=ecol0 // tc, e=e_off: (i + e, j + c)))
        args.append(arr)
    alias = _into(into, args, in_specs)
    if into is not None:
        assert into.shape[1] == n_cols and into.dtype == out_dtype
        out_rows = into.shape[0]
    return pl.pallas_call(
        functools.partial(_conv_body, k_width, zero_first, epilogue, len(extras)),
        grid=(rows // tr, pl.cdiv(n_cols, tc)),
        in_specs=in_specs,
        out_specs=pl.BlockSpec((tr, tc), lambda i, j: (i + o_off, j)),
        out_shape=jax.ShapeDtypeStruct((out_rows, n_cols), out_dtype),
        input_output_aliases=alias,
        scratch_shapes=[pltpu.VMEM((tr + SUBLANES, tc), F32)],
        compiler_params=_params(("parallel", "parallel")),
        name=name,
    )(*args)


def _dn_act(dn_width, y):
    y = _silu(y)
    if y.shape[1] == 3 * dn_width:
        outs = []
        for h in range(3 * dn_width // HEAD_DIM):
            yh = y[:, h * HEAD_DIM:(h + 1) * HEAD_DIM]
            sec = h * HEAD_DIM // dn_width
            if sec < 2:
                ss = jnp.sum(yh * yh, axis=-1, keepdims=True)
                yh = yh * lax.rsqrt(ss + EPS) * (HEAD_DIM ** -0.5 if sec == 0 else 1.0)
            outs.append(yh)
        return jnp.concatenate(outs, axis=-1)
    sec = pl.program_id(1)
    scale = jnp.where(sec == 0, HEAD_DIM ** -0.5, 1.0).astype(F32)
    outs = []
    for h in range(dn_width // HEAD_DIM):
        yh = y[:, h * HEAD_DIM:(h + 1) * HEAD_DIM]
        ss = jnp.sum(yh * yh, axis=-1, keepdims=True)
        yn = yh * lax.rsqrt(ss + EPS) * scale
        outs.append(jnp.where(sec < 2, yn, yh))
    return jnp.concatenate(outs, axis=-1)


def _ffn_act(y, up):
    return _silu(y) * up


def _gate_body(chunk, ba_ref, alog_ref, dtb_ref, beta_ref, g_ref):
    x = ba_ref[...]
    tr = x.shape[0]
    beta_ref[...] = _sigmoid(x)
    z = x + dtb_ref[...]
    softplus = jnp.maximum(z, 0.0) + jnp.log1p(jnp.exp(-jnp.abs(z)))
    g = -jnp.exp(alog_ref[...]) * softplus
    i = lax.broadcasted_iota(jnp.int32, (tr, tr), 0)
    j = lax.broadcasted_iota(jnp.int32, (tr, tr), 1)
    tri = jnp.where((i // chunk == j // chunk) & (j <= i), 1.0, 0.0).astype(F32)
    g_ref[...] = jnp.dot(tri, g, preferred_element_type=F32, precision=lax.Precision.HIGHEST)


def _gates(ba, a_log, dt_bias, *, rows, row0, chunk, tr):
    n_heads = a_log.shape[0]
    pad = lambda v: jnp.zeros((1, LANES), F32).at[0, n_heads:2 * n_heads].set(v.astype(F32))
    off = row0 // tr
    assert row0 % tr == 0 and rows % tr == 0 and tr % chunk == 0
    return pl.pallas_call(
        functools.partial(_gate_body, chunk),
        grid=(rows // tr,),
        in_specs=[pl.BlockSpec((tr, LANES), lambda i: (i + off, 0)),
                  pl.BlockSpec((1, LANES), lambda i: (0, 0)),
                  pl.BlockSpec((1, LANES), lambda i: (0, 0))],
        out_specs=[pl.BlockSpec((tr, LANES), lambda i: (i, 0)),
                   pl.BlockSpec((tr, LANES), lambda i: (i, 0))],
        out_shape=[jax.ShapeDtypeStruct((rows, LANES), F32)] * 2,
        compiler_params=_params(("parallel",)),
        name="dn_gates",
    )(ba, pad(a_log), pad(dt_bias))


def _split3(x):
    hi = x.astype(BF16)
    return hi, (x - hi.astype(F32)).astype(BF16)


def _dot3(a, b):
    lhs = jnp.concatenate([a[0], a[0], a[1]], axis=1)
    rhs = jnp.concatenate([b[0], b[1], b[0]], axis=0)
    return jnp.dot(lhs, rhs, preferred_element_type=F32)


def _unit_lower_inverse(a_mats, ii, jj, chunk):
    base = min(chunk, 16)
    eye = jnp.where(ii == jj, 1.0, 0.0).astype(F32)
    in_base = ii // base == jj // base
    ds = [jnp.where(in_base, a, 0.0) for a in a_mats]
    ps = [eye - d for d in ds]
    dps = [_split3(d) for d in ds]
    for _ in range(int(math.log2(base)) - 1):
        dps = [_split3(_dot3(dp, dp)) for dp in dps]
        ps = [p + _dot3(_split3(p), dp) for p, dp in zip(ps, dps)]
    size = base
    while size < chunk:
        below = (ii // (2 * size) == jj // (2 * size)) & (ii // size != jj // size)
        es = [_split3(jnp.where(below, a, 0.0)) for a in a_mats]
        pss = [_split3(p) for p in ps]
        ts = [_split3(_dot3(p_s, e)) for p_s, e in zip(pss, es)]
        ps = [p - _dot3(t, p_s) for p, t, p_s in zip(ps, ts, pss)]
        size *= 2
    return ps


def _dn_local_body(chunk, n_heads, hb, q_ref, k_ref, v_ref, beta_ref, g_ref,
                   u_ref, w_ref, qd_ref, kdt_ref, aqk_ref, gl_ref):
    n = q_ref.shape[0]
    hg = pl.program_id(0)
    ii = lax.broadcasted_iota(jnp.int32, (n, n), 0)
    jj = lax.broadcasted_iota(jnp.int32, (n, n), 1)
    same = ii // chunk == jj // chunk
    causal = same & (ii >= jj)
    strict = same & (ii > jj)
    last = jj == (ii // chunk) * chunk + (chunk - 1)
    beta_all = beta_ref[...]
    g_all = g_ref[...]
    lane = lax.broadcasted_iota(jnp.int32, (n, LANES), 1)
    nt = (((1,), (1,)), ((), ()))
    a_mats, rhss = [], []
    for hh in range(hb):
        h = hg * hb + hh
        sl = slice(hh * HEAD_DIM, (hh + 1) * HEAD_DIM)
        q, k, v = q_ref[:, sl], k_ref[:, sl], v_ref[:, sl]
        beta = jnp.sum(jnp.where(lane == h, beta_all, 0.0), axis=-1, keepdims=True)
        gcum = jnp.sum(jnp.where(lane == h + n_heads, g_all, 0.0), axis=-1, keepdims=True)
        g_rows = jnp.broadcast_to(gcum, (n, n))
        g_cols = g_rows.T
        g_last = jnp.sum(jnp.where(last, g_cols, 0.0), axis=-1, keepdims=True)
        gamma = jnp.exp(jnp.where(causal, g_rows - g_cols, -jnp.inf))
        kb = k.astype(BF16)
        kk = lax.dot_general(kb, kb, nt, preferred_element_type=F32)
        qk = lax.dot_general(q.astype(BF16), kb, nt, preferred_element_type=F32)
        a_mats.append(jnp.where(strict, beta * kk * gamma, 0.0))
        eg = jnp.exp(gcum)
        rhss.append(_split3(jnp.concatenate([v * beta, k * (beta * eg)], axis=-1)))
        qd_ref[hh] = q * eg
        kdt_ref[hh] = (k * jnp.exp(g_last - gcum)).T
        aqk_ref[hh] = qk * gamma
        gl_ref[hh] = jnp.broadcast_to(jnp.exp(g_last), (n, LANES))
    invs = _unit_lower_inverse(a_mats, ii, jj, chunk)
    for hh in range(hb):
        sol = _dot3(_split3(invs[hh]), rhss[hh])
        u_ref[hh] = sol[:, :HEAD_DIM]
        w_ref[hh] = sol[:, HEAD_DIM:]


def _dn_local(qkv, beta, gcum, *, rows, row0, chunk, n_heads, dn_width, hb, n):
    assert rows % n == 0 and row0 % n == 0 and n % chunk == 0
    off = row0 // n
    bw = hb * HEAD_DIM
    sec = dn_width // bw
    qkv_spec = lambda s: pl.BlockSpec((n, bw), lambda g, i, s=s: (i + off, g + s * sec))
    head_spec = lambda: pl.BlockSpec((hb, n, HEAD_DIM), lambda g, i: (g, i, 0))
    gate_spec = pl.BlockSpec((n, LANES), lambda g, i: (i, 0))
    shp = lambda *s: jax.ShapeDtypeStruct(s, F32)
    return pl.pallas_call(
        functools.partial(_dn_local_body, chunk, n_heads, hb),
        grid=(n_heads // hb, rows // n),
        in_specs=[qkv_spec(0), qkv_spec(1), qkv_spec(2), gate_spec, gate_spec],
        out_specs=[head_spec(), head_spec(), head_spec(),
                   pl.BlockSpec((hb, HEAD_DIM, n), lambda g, i: (g, 0, i)),
                   pl.BlockSpec((hb, n, n), lambda g, i: (g, i, 0)),
                   head_spec()],
        out_shape=[shp(n_heads, rows, HEAD_DIM), shp(n_heads, rows, HEAD_DIM),
                   shp(n_heads, rows, HEAD_DIM), shp(n_heads, HEAD_DIM, rows),
                   shp(n_heads, rows, n), shp(n_heads, rows, LANES)],
        compiler_params=_params(("parallel", "parallel")),
        name="dn_local",
    )(qkv, qkv, qkv, beta, gcum)


def _gated_head_norm(o, z, gain):
    ms = jnp.mean(o * o, axis=-1, keepdims=True)
    return ((o * lax.rsqrt(ms + EPS)) * gain) * _silu(z)


def _dn_scan_body(chunk, n_heads, u_ref, w_ref, qd_ref, kdt_ref, aqk_ref, gl_ref, z_ref, gain_ref,
                  o_ref, s_out_ref, s_ref):
    step = pl.program_id(0)
    n = u_ref.shape[1]

    @pl.when(step == 0)
    def _():
        s_ref[...] = jnp.zeros_like(s_ref)

    gain = gain_ref[...]
    for c in range(n // chunk):
        rows = slice(c * chunk, (c + 1) * chunk)
        heads = range(n_heads)
        ps = [jnp.dot(jnp.concatenate([w_ref[h, rows, :], qd_ref[h, rows, :]], axis=0).astype(BF16),
                      s_ref[h].astype(BF16), preferred_element_type=F32) for h in heads]
        vbs = [(u_ref[h, rows, :] - ps[h][:chunk]).astype(BF16) for h in heads]
        for h in heads:
            s_ref[h] = s_ref[h] * gl_ref[h, c * chunk:c * chunk + 1, :] + jnp.dot(
                kdt_ref[h, :, rows].astype(BF16), vbs[h], preferred_element_type=F32)
        for h in heads:
            o = ps[h][chunk:] + jnp.dot(aqk_ref[h, rows, rows].astype(BF16), vbs[h],
                                        preferred_element_type=F32)
            cols = slice(h * HEAD_DIM, (h + 1) * HEAD_DIM)
            o_ref[rows, cols] = _gated_head_norm(o, z_ref[rows, cols], gain).astype(o_ref.dtype)

    @pl.when(step == pl.num_programs(0) - 1)
    def _():
        s_out_ref[...] = s_ref[...]


def _dn_scan(u, w, qd, kdt, aqk, gl, zsrc, z_col0, gain, *, chunk, n, out_rows):
    n_heads, rows, _ = u.shape
    dn_width = n_heads * HEAD_DIM
    assert z_col0 % dn_width == 0
    head_spec = lambda: pl.BlockSpec((n_heads, n, HEAD_DIM), lambda i: (0, i, 0))
    return pl.pallas_call(
        functools.partial(_dn_scan_body, chunk, n_heads),
        grid=(rows // n,),
        in_specs=[head_spec(), head_spec(), head_spec(),
                  pl.BlockSpec((n_heads, HEAD_DIM, n), lambda i: (0, 0, i)),
                  pl.BlockSpec((n_heads, n, n), lambda i: (0, i, 0)),
                  head_spec(),
                  pl.BlockSpec((n, dn_width), lambda i: (i, z_col0 // dn_width)),
                  pl.BlockSpec((1, HEAD_DIM), lambda i: (0, 0))],
        out_specs=[pl.BlockSpec((n, dn_width), lambda i: (i, 0)),
                   pl.BlockSpec((n_heads, HEAD_DIM, HEAD_DIM), lambda i: (0, 0, 0))],
        out_shape=[jax.ShapeDtypeStruct((out_rows, dn_width), BF16),
                   jax.ShapeDtypeStruct((n_heads, HEAD_DIM, HEAD_DIM), F32)],
        scratch_shapes=[pltpu.VMEM((n_heads, HEAD_DIM, HEAD_DIM), F32)],
        compiler_params=_params(("arbitrary",)),
        name="dn_scan",
    )(u, w, qd, kdt, aqk, gl, zsrc, gain.reshape(1, HEAD_DIM))


def _dn_sample_body(t_len, u_ref, w_ref, qd_ref, kdt_ref, aqk_ref, gl_ref, z_ref, gain_ref, s_ref,
                    into_ref, o_ref, s_out_ref):
    n = u_ref.shape[1]
    nb = n // t_len
    s = s_ref[0, :, 0]
    w3 = w_ref[0].reshape(nb, t_len, HEAD_DIM)
    q3 = qd_ref[0].reshape(nb, t_len, HEAD_DIM)
    wq = jnp.concatenate([w3, q3], axis=1).astype(BF16)
    p = jnp.einsum('bck,bkd->bcd', wq, s.astype(BF16), preferred_element_type=F32)
    ws = p[:, :t_len].reshape(n, HEAD_DIM)
    qs = p[:, t_len:].reshape(n, HEAD_DIM)
    v_new = u_ref[0] - ws
    vb = v_new.astype(BF16)
    o = qs + jnp.dot(aqk_ref[0].astype(BF16), vb, preferred_element_type=F32)
    kdt = kdt_ref[0]
    ri = lax.broadcasted_iota(jnp.int32, (nb * HEAD_DIM, n), 0)
    ci = lax.broadcasted_iota(jnp.int32, (nb * HEAD_DIM, n), 1)
    zt = jnp.where(ri // HEAD_DIM == ci // t_len, jnp.tile(kdt, (nb, 1)), 0.0).astype(BF16)
    upd = jnp.dot(zt, vb, preferred_element_type=F32).reshape(nb, HEAD_DIM, HEAD_DIM)
    gl = gl_ref[0].reshape(nb, t_len, LANES)[:, 0:1, :]
    s_out_ref[0, :, 0] = s * gl + upd
    o_ref[...] = _gated_head_norm(o, z_ref[...], gain_ref[...]).astype(o_ref.dtype)


def _dn_sample(u, w, qd, kdt, aqk, gl, zsrc, z_row0, z_col0, gain, state, into, *, t_len):
    n_heads, n, _ = u.shape
    nb = n // t_len
    assert z_row0 % n == 0 and z_col0 % HEAD_DIM == 0
    head_spec = lambda: pl.BlockSpec((1, n, HEAD_DIM), lambda h: (h, 0, 0))
    state_spec = pl.BlockSpec((1, nb, 1, HEAD_DIM, HEAD_DIM), lambda h: (0, 0, h, 0, 0))
    in_specs = [head_spec(), head_spec(), head_spec(),
                pl.BlockSpec((1, HEAD_DIM, n), lambda h: (h, 0, 0)),
                pl.BlockSpec((1, n, n), lambda h: (h, 0, 0)),
                head_spec(),
                pl.BlockSpec((n, HEAD_DIM), lambda h: (z_row0 // n, z_col0 // HEAD_DIM + h)),
                pl.BlockSpec((1, HEAD_DIM), lambda h: (0, 0)),
                state_spec]
    args = [u, w, qd, kdt, aqk, gl, zsrc, gain.reshape(1, HEAD_DIM), state]
    alias = _into(into, args, in_specs)
    return pl.pallas_call(
        functools.partial(_dn_sample_body, t_len),
        grid=(n_heads,),
        in_specs=in_specs,
        out_specs=[pl.BlockSpec((n, HEAD_DIM), lambda h: (z_row0 // n, h)), state_spec],
        out_shape=[jax.ShapeDtypeStruct(into.shape, into.dtype),
                   jax.ShapeDtypeStruct(state.shape, F32)],
        input_output_aliases=alias,
        compiler_params=_params(("parallel",)),
        name="dn_sample",
    )(*args)


def _rope_body(n_heads, dil, q_ref, k_ref, v_ref, cos_ref, sin_ref, qo_ref, kvo_ref, slab):
    width = n_heads * HEAD_DIM
    per = q_ref.shape[0] // dil
    cos, sin = cos_ref[...], sin_ref[...]

    def regroup(x, dst_ref, dst_sl):
        if dil == 1:
            dst_ref[0, :, dst_sl] = x.astype(dst_ref.dtype)
            return
        slab[...] = x

        def one_residue(r, carry):
            dst_ref[r, :, dst_sl] = slab[pl.ds(r, per, stride=dil), :].astype(dst_ref.dtype)
            return carry

        lax.fori_loop(0, dil, one_residue, 0)

    for h in range(n_heads):
        sl = slice(h * HEAD_DIM, (h + 1) * HEAD_DIM)
        q = q_ref[:, sl]
        k = k_ref[:, sl]
        regroup(q * cos + pltpu.roll(q, HEAD_DIM // 2, 1) * sin, qo_ref, sl)
        regroup(k * cos + pltpu.roll(k, HEAD_DIM // 2, 1) * sin, kvo_ref, sl)
        regroup(v_ref[:, sl], kvo_ref, slice(width + h * HEAD_DIM, width + (h + 1) * HEAD_DIM))


def _rope(att_src, col0, cos, sin, *, rows, row0, n_heads, tr, dil, q_dtype):
    width = n_heads * HEAD_DIM
    assert row0 % tr == 0 and rows % tr == 0 and col0 % width == 0 and tr % (16 * dil) == 0
    r_off, c_off = row0 // tr, col0 // width
    per = tr // dil
    src = lambda s: pl.BlockSpec((tr, width), lambda i, s=s: (i + r_off, c_off + s))
    tab = pl.BlockSpec((tr, HEAD_DIM), lambda i: (i, 0))
    return pl.pallas_call(
        functools.partial(_rope_body, n_heads, dil),
        grid=(rows // tr,),
        in_specs=[src(0), src(1), src(2), tab, tab],
        out_specs=[pl.BlockSpec((dil, per, width), lambda i: (0, i, 0)),
                   pl.BlockSpec((dil, per, 2 * width), lambda i: (0, i, 0))],
        out_shape=[jax.ShapeDtypeStruct((dil, rows // dil, width), q_dtype),
                   jax.ShapeDtypeStruct((dil, rows // dil, 2 * width), F32)],
        scratch_shapes=[pltpu.VMEM((tr, HEAD_DIM), F32)],
        compiler_params=_params(("parallel",)),
        name="rope",
    )(att_src, att_src, att_src, cos, sin)


def _rope_tables(pos):
    half = HEAD_DIM // 2
    inv_freq = ROPE_THETA ** (-jnp.arange(half, dtype=F32) / half)
    ang = pos.astype(F32)[:, None] * inv_freq[None, :]
    cos, sin = jnp.cos(ang), jnp.sin(ang)
    return jnp.concatenate([cos, cos], axis=-1), jnp.concatenate([-sin, sin], axis=-1)


def _attn_prompt_body(n_heads, dil, q_ref, kc_ref, kp_ref, vc_ref, vp_ref, o_ref, lse_ref):
    nb = ATT_BLOCK
    blk = pl.program_id(0)
    r = pl.program_id(1)
    rows = pl.ds(r, nb, stride=dil) if dil > 1 else slice(None)
    i = lax.broadcasted_iota(jnp.int32, (nb, 2 * nb), 0)
    j = lax.broadcasted_iota(jnp.int32, (nb, 2 * nb), 1)
    mask = (j >= i) & (j <= i + nb) & ((blk > 0) | (j >= nb))
    lane = lax.broadcasted_iota(jnp.int32, (nb, LANES), 1)
    scale = HEAD_DIM ** -0.5
    lse_all = jnp.zeros((nb, LANES), F32)
    for h in range(n_heads):
        sl = slice(h * HEAD_DIM, (h + 1) * HEAD_DIM)
        k = jnp.concatenate([kp_ref[:, sl], kc_ref[:, sl]], axis=0).astype(BF16)
        v = jnp.concatenate([vp_ref[:, sl], vc_ref[:, sl]], axis=0).astype(BF16)
        s = lax.dot_general(q_ref[:, sl], k, (((1,), (1,)), ((), ())), preferred_element_type=F32) * scale
        s = jnp.where(mask, s, -jnp.inf)
        m = jnp.max(s, axis=-1, keepdims=True)
        p = jnp.exp(s - m)
        l = jnp.sum(p, axis=-1, keepdims=True)
        o_ref[h, rows, :] = jnp.dot(p.astype(BF16), v, preferred_element_type=F32) / l
        lse_all = jnp.where(lane == h, m + jnp.log(l), lse_all)
    lse_ref[rows, :] = lse_all


def _attn_prompt(q, kv, dil, n_heads):
    _, m_len, width = q.shape
    nb = ATT_BLOCK
    t_len = m_len * dil
    assert m_len % nb == 0
    src = lambda f: pl.BlockSpec((None, nb, width), f)
    prev = lambda n: jnp.maximum(n - 1, 0)
    return pl.pallas_call(
        functools.partial(_attn_prompt_body, n_heads, dil),
        grid=(m_len // nb, dil),
        in_specs=[src(lambda n, r: (r, n, 0)),
                  src(lambda n, r: (r, n, 0)), src(lambda n, r: (r, prev(n), 0)),
                  src(lambda n, r: (r, n, 1)), src(lambda n, r: (r, prev(n), 1))],
        out_specs=[pl.BlockSpec((n_heads, nb * dil, HEAD_DIM), lambda n, r: (0, n, 0)),
                   pl.BlockSpec((nb * dil, LANES), lambda n, r: (n, 0))],
        out_shape=[jax.ShapeDtypeStruct((n_heads, t_len, HEAD_DIM), F32),
                   jax.ShapeDtypeStruct((t_len, LANES), F32)],
        compiler_params=_params(("parallel", "arbitrary")),
        name="attn_prompt",
    )(q, kv, kv, kv, kv)


def _attn_sample_body(n_heads, dil, stride, q_ref, kvn_ref, cache_ref, o_ref, lse_ref):
    t_len = q_ref.shape[1]
    width = n_heads * HEAD_DIM
    n_buf = math.prod(cache_ref.shape[1:-3])
    n_pair = n_buf * n_heads
    lg_h, lg_t = int(math.log2(n_heads)), int(math.log2(t_len))
    assert (1 << lg_h) == n_heads and (1 << lg_t) == t_len and dil & (dil - 1) == 0

    def cache_part(kv):
        if stride > 1:
            x = cache_ref[0, :, :, kv]
        else:
            x = cache_ref[0, :, kv]
        return x.reshape(n_pair, HEAD_DIM).astype(BF16)

    def heads_on_rows(ref, col0):
        return jnp.concatenate(
            [ref[0, :, col0 + h * HEAD_DIM:col0 + (h + 1) * HEAD_DIM] for h in range(n_heads)],
            axis=0).astype(BF16)

    q_all = heads_on_rows(q_ref, 0)
    k_new = heads_on_rows(kvn_ref, 0)
    v_new = heads_on_rows(kvn_ref, width)
    rq = lax.broadcasted_iota(jnp.int32, (n_heads * t_len, n_pair), 0)
    cc = lax.broadcasted_iota(jnp.int32, (n_heads * t_len, n_pair), 1)
    jq, hq = rq & (t_len - 1), rq >> lg_t
    row, hc = cc >> lg_h, cc & (n_heads - 1)
    if stride > 1:
        idx = (row >> lg_t) * stride + (row & (t_len - 1))
    else:
        idx = row
    mask_buf = (hq == hc) & (idx >= jq) & (((idx - jq) & (dil - 1)) == 0)
    rn = lax.broadcasted_iota(jnp.int32, (n_heads * t_len, n_heads * t_len), 0)
    cn = lax.broadcasted_iota(jnp.int32, (n_heads * t_len, n_heads * t_len), 1)
    jn, jc = rn & (t_len - 1), cn & (t_len - 1)
    mask_new = ((rn >> lg_t) == (cn >> lg_t)) & (jc <= jn) & (((jn - jc) & (dil - 1)) == 0)
    scale = HEAD_DIM ** -0.5
    nt = (((1,), (1,)), ((), ()))
    s_buf = lax.dot_general(q_all, cache_part(0), nt, preferred_element_type=F32) * scale
    s_new = lax.dot_general(q_all, k_new, nt, preferred_element_type=F32) * scale
    s_buf = jnp.where(mask_buf, s_buf, -jnp.inf)
    s_new = jnp.where(mask_new, s_new, -jnp.inf)
    m = jnp.maximum(jnp.max(s_buf, axis=-1, keepdims=True), jnp.max(s_new, axis=-1, keepdims=True))
    p_buf = jnp.exp(s_buf - m)
    p_new = jnp.exp(s_new - m)
    l = jnp.sum(p_buf, axis=-1, keepdims=True) + jnp.sum(p_new, axis=-1, keepdims=True)
    o = (jnp.dot(p_buf.astype(BF16), cache_part(1), preferred_element_type=F32)
         + jnp.dot(p_new.astype(BF16), v_new, preferred_element_type=F32)) / l
    lse = m + jnp.log(l)
    lane = lax.broadcasted_iota(jnp.int32, (t_len, LANES), 1)
    lse_all = jnp.zeros((t_len, LANES), F32)
    for h in range(n_heads):
        o_ref[h] = o[h * t_len:(h + 1) * t_len]
        lse_all = jnp.where(lane == h, lse[h * t_len:(h + 1) * t_len], lse_all)
    lse_ref[...] = lse_all


def _attn_sample(q, kvn, cache, win, dil, n_heads, t_len):
    bsz, buf_len = cache.shape[:2]
    width = n_heads * HEAD_DIM
    assert buf_len == win and win == ATT_BLOCK * dil, "window buffer must be full"
    if dil >= 2 * t_len:
        stride = dil
        cache_v = cache.reshape(bsz, buf_len // dil, dil, 2, n_heads, HEAD_DIM)
        cache_spec = pl.BlockSpec((1, buf_len // dil, t_len, 2, n_heads, HEAD_DIM),
                                  lambda b: (b, 0, 0, 0, 0, 0))
    else:
        stride = 1
        cache_v = cache
        cache_spec = pl.BlockSpec((1, buf_len, 2, n_heads, HEAD_DIM), lambda b: (b, 0, 0, 0, 0))
    return pl.pallas_call(
        functools.partial(_attn_sample_body, n_heads, dil, stride),
        grid=(bsz,),
        in_specs=[pl.BlockSpec((1, t_len, width), lambda b: (0, b, 0)),
                  pl.BlockSpec((1, t_len, 2 * width), lambda b: (0, b, 0)),
                  cache_spec],
        out_specs=[pl.BlockSpec((n_heads, t_len, HEAD_DIM), lambda b: (0, b, 0)),
                   pl.BlockSpec((t_len, LANES), lambda b: (b, 0))],
        out_shape=[jax.ShapeDtypeStruct((n_heads, bsz * t_len, HEAD_DIM), F32),
                   jax.ShapeDtypeStruct((bsz * t_len, LANES), F32)],
        compiler_params=_params(("parallel",)),
        name="attn_sample",
    )(q, kvn, cache_v)


def _merge_body(n_heads, o0, o1, o2, l0, l1, l2, *refs):
    out_ref = refs[-1]
    la, lb, lc = l0[...], l1[...], l2[...]
    m = jnp.maximum(jnp.maximum(la, lb), lc)
    ea, eb, ec = jnp.exp(la - m), jnp.exp(lb - m), jnp.exp(lc - m)
    tot = ea + eb + ec
    wa, wb, wc = ea / tot, eb / tot, ec / tot
    for h in range(n_heads):
        sl = slice(h * HEAD_DIM, (h + 1) * HEAD_DIM)
        col = slice(h, h + 1)
        out_ref[:, sl] = (wa[:, col] * o0[h] + wb[:, col] * o1[h]
                          + wc[:, col] * o2[h]).astype(out_ref.dtype)


def _merge(outs, lses, tr, *, out_rows=None, row0=0, into=None):
    n_heads, rows, _ = outs[0].shape
    width = n_heads * HEAD_DIM
    assert row0 % tr == 0 and rows % tr == 0
    off = row0 // tr
    spec = pl.BlockSpec((n_heads, tr, HEAD_DIM), lambda i: (0, i, 0))
    lspec = pl.BlockSpec((tr, LANES), lambda i: (i, 0))
    in_specs = [spec] * 3 + [lspec] * 3
    args = [*outs, *lses]
    alias = _into(into, args, in_specs)
    out_rows = into.shape[0] if into is not None else (rows if out_rows is None else out_rows)
    return pl.pallas_call(
        functools.partial(_merge_body, n_heads),
        grid=(rows // tr,),
        in_specs=in_specs,
        out_specs=pl.BlockSpec((tr, width), lambda i: (i + off, 0)),
        out_shape=jax.ShapeDtypeStruct((out_rows, width), BF16),
        input_output_aliases=alias,
        compiler_params=_params(("parallel",)),
        name="merge_groups",
    )(*args)


def _residual_rows(n_head, xp, xs):
    last = pl.program_id(1) == pl.num_programs(1) - 1
    return jnp.where(last, jnp.concatenate([xp[:n_head], xs], axis=0), xp)


def _layer(x_p, x_s, bsz, t_s, past_len, caches, dn_state, dn_conv_state, ffn_conv_state,
           norm_mix, w_in, dn_conv_w, dn_a_log, dn_dt_bias, dn_out_norm, w_branch_dn,
           w_branch_att, w_out, norm_ffn, w_ffn_gate, w_ffn_up, ffn_conv_w, w_ffn_down):
    t_p, d_model = x_p.shape
    n_s = bsz * t_s
    r_all = t_p + n_s
    n_heads_dn = dn_a_log.shape[0]
    dn_width = n_heads_dn * HEAD_DIM
    n_groups = len(ATT_GROUPS)
    att_width = w_branch_att.shape[0]
    n_heads_att = att_width // HEAD_DIM
    d_ff = w_ffn_gate.shape[1]
    tm = _pick_tile(r_all, 1100, 16)

    c_ba = 4 * dn_width
    c_att = c_ba + 2 * n_heads_dn
    n1 = _rmsnorm(x_p, norm_mix, BF16, out_rows=r_all)
    n1 = _rmsnorm(x_s, norm_mix, BF16, out_row0=t_p, into=n1)
    w_in_t = w_in.T
    proj = functools.partial(_matmul_w, n1, w_in_t, transposed=True, out_dtype=F32, tm=tm)
    qkvz = proj(col0=0, n=c_ba, tn=512, name="proj_dn")
    ba = proj(col0=c_ba, n=LANES, tn=LANES, name="proj_ba")
    rest = proj(col0=c_att, tn=512, name="proj_att")
    col_gdn = 3 * n_groups * att_width
    col_gatt = col_gdn + d_model

    dn_act = functools.partial(_dn_act, dn_width)
    conv_args = dict(k_width=DN_CONV, tc=dn_width, col0=0, n_cols=3 * dn_width,
                     epilogue=dn_act, out_dtype=F32)
    qkv_p = _conv(qkvz, None, dn_conv_w, rows=t_p, row0=0, tr=512, zero_first=True,
                  name="dn_conv_prompt", **conv_args)
    state_pad = jnp.pad(dn_conv_state, ((0, 0), (SUBLANES - (DN_CONV - 1), 0), (0, 0)))
    conv_args["tc"] = 3 * dn_width
    qkv_s = _conv(qkvz, state_pad.reshape(bsz * SUBLANES, -1), dn_conv_w, rows=n_s, row0=t_p,
                  tr=t_s, zero_first=False, name="dn_conv_sample", **conv_args)
    p_dn_conv = qkvz[t_p - (DN_CONV - 1):t_p, :3 * dn_width][None]
    s_dn_conv = qkvz[t_p:].reshape(bsz, t_s, -1)[:, t_s - (DN_CONV - 1):, :3 * dn_width]

    beta_p, g_p = _gates(ba, dn_a_log, dn_dt_bias, rows=t_p, row0=0, chunk=DN_CHUNK, tr=512)
    beta_s, g_s = _gates(ba, dn_a_log, dn_dt_bias, rows=n_s, row0=t_p, chunk=t_s, tr=n_s)

    loc_p = _dn_local(qkv_p, beta_p, g_p, rows=t_p, row0=0, chunk=DN_CHUNK,
                      n_heads=n_heads_dn, dn_width=dn_width, hb=8, n=128)
    o_dn, p_dn = _dn_scan(*loc_p, qkvz, 3 * dn_width, dn_out_norm, chunk=DN_CHUNK, n=128,
                          out_rows=r_all)
    loc_s = _dn_local(qkv_s, beta_s, g_s, rows=n_s, row0=0, chunk=t_s,
                      n_heads=n_heads_dn, dn_width=dn_width, hb=4, n=n_s)
    o_dn, s_dn = _dn_sample(*loc_s, qkvz, t_p, 3 * dn_width, dn_out_norm, dn_state, o_dn, t_len=t_s)

    cos_p, sin_p = _rope_tables(jnp.arange(t_p, dtype=jnp.int32))
    cos_s, sin_s = _rope_tables(past_len + jnp.tile(jnp.arange(t_s, dtype=jnp.int32), bsz))
    outs_p, lses_p, outs_s, lses_s, p_kv, s_kv = [], [], [], [], [], []
    for gi, (win, dil) in enumerate(ATT_GROUPS):
        col = 3 * gi * att_width
        q_p, kv_p = _rope(rest, col, cos_p, sin_p, rows=t_p, row0=0, n_heads=n_heads_att,
                          tr=1024, dil=dil, q_dtype=BF16)
        q_s, kv_s = _rope(rest, col, cos_s, sin_s, rows=n_s, row0=t_p, n_heads=n_heads_att,
                          tr=n_s, dil=1, q_dtype=F32)
        o, lse = _attn_prompt(q_p, kv_p, dil, n_heads_att)
        outs_p.append(o)
        lses_p.append(lse)
        o, lse = _attn_sample(q_s, kv_s, caches[gi], win, dil, n_heads_att, t_s)
        outs_s.append(o)
        lses_s.append(lse)
        keep = min(win, t_p)
        tail = kv_p[:, (t_p - keep) // dil:].transpose(1, 0, 2)
        p_kv.append(tail.reshape(1, keep, 2, n_heads_att, HEAD_DIM))
        s_kv.append(kv_s.reshape(bsz, t_s, 2, n_heads_att, HEAD_DIM))
    o_att = _merge(outs_p, lses_p, 512, out_rows=r_all)
    o_att = _merge(outs_s, lses_s, n_s, row0=t_p, into=o_att)

    y_dn = _matmul_w(o_dn, w_branch_dn, out_dtype=F32, tm=tm, tn=512,
                     epilogue=lambda acc, g: _sigmoid(g) * acc, extras=[(rest, col_gdn)],
                     name="branch_dn")
    mix = _matmul_w(o_att, w_branch_att, out_dtype=BF16, tm=tm, tn=512,
                    epilogue=lambda acc, g, y: y + _sigmoid(g) * acc,
                    extras=[(rest, col_gatt), (y_dn, 0)], name="branch_att")
    n_head = t_p - (r_all - tm)
    assert 0 <= n_head and n_head % SUBLANES == 0 and n_s < tm, "sample rows must share the last row tile"
    x1 = _matmul_w(mix, w_out, out_dtype=F32, tm=tm, tn=512,
                   epilogue=lambda acc, xp, xs: _residual_rows(n_head, xp, xs) + acc,
                   extras=[(x_p, 0), (x_s, 0)], name="out_proj")

    n2 = _rmsnorm(x1, norm_ffn, BF16)
    gate = _matmul_w(n2, w_ffn_gate, out_dtype=F32, tm=tm, tn=512, name="ffn_gate")
    h = _ffn_up(n2, w_ffn_up, gate, ffn_conv_w, k_width=FFN_CONV, tm=tm, tn=512)
    up_s = _matmul_w(n2, w_ffn_up, out_dtype=F32, tm=n_s, tn=512, row0=t_p, rows=n_s, name="ffn_up_sample")
    fstate_pad = jnp.pad(ffn_conv_state, ((0, 0), (SUBLANES - (FFN_CONV - 1), 0), (0, 0)))
    h = _conv(gate, fstate_pad.reshape(bsz * SUBLANES, -1), ffn_conv_w, k_width=FFN_CONV,
              rows=n_s, row0=t_p, tr=t_s, tc=d_ff, col0=0, n_cols=d_ff, zero_first=False,
              epilogue=_ffn_act, extras=[(up_s, 0)], out_dtype=BF16, name="ffn_conv_sample", into=h)
    p_ffn_conv = gate[t_p - (FFN_CONV - 1):t_p][None]
    s_ffn_conv = gate[t_p:].reshape(bsz, t_s, d_ff)[:, t_s - (FFN_CONV - 1):]
    tm_down = _pick_tile(r_all, 600, 16)
    x2 = _matmul(h, w_ffn_down.astype(BF16), out_dtype=F32, tm=tm_down, tn=512,
                 epilogue=lambda acc, r: r + acc, extras=[(x1, 0)], name="ffn_down")
    states_p = (p_kv[0], p_kv[1], p_kv[2], p_dn[None], p_dn_conv, p_ffn_conv)
    states_s = (s_kv[0], s_kv[1], s_kv[2], s_dn[0], s_dn_conv, s_ffn_conv)
    return x2, states_p, states_s


def kernel(x_prompt, x_sample, cache_kv_w128, cache_kv_w512, cache_kv_w2048, state_dn, state_dn_conv, state_ffn_conv, norm_mix, w_in, dn_conv_w, dn_a_log, dn_dt_bias, dn_out_norm, w_branch_dn, w_branch_att, w_out, norm_ffn, w_ffn_gate, w_ffn_up, ffn_conv_w, w_ffn_down, norm_final):
    b_p, t_p, d_model = x_prompt.shape
    bsz, t_s, _ = x_sample.shape
    depth = w_in.shape[0]
    assert b_p == 1 and depth == 1, "one prompt sequence, one layer"
    l = 0
    x, st_p, st_s = _layer(
        x_prompt.reshape(t_p, d_model), x_sample.reshape(bsz * t_s, d_model), bsz, t_s, PAST_LEN,
        (cache_kv_w128[l], cache_kv_w512[l], cache_kv_w2048[l]),
        state_dn[l:l + 1], state_dn_conv[l], state_ffn_conv[l],
        norm_mix[l], w_in[l], dn_conv_w[l], dn_a_log[l], dn_dt_bias[l], dn_out_norm[l],
        w_branch_dn[l], w_branch_att[l], w_out[l], norm_ffn[l], w_ffn_gate[l], w_ffn_up[l],
        ffn_conv_w[l], w_ffn_down[l])
    y_prompt = _rmsnorm(x, norm_final, F32, row0=0, rows=t_p).reshape(1, t_p, d_model)
    y_sample = _rmsnorm(x, norm_final, F32, row0=t_p, rows=bsz * t_s).reshape(bsz, t_s, d_model)
    return (y_prompt, y_sample) + tuple(s[None] for s in st_p) + tuple(s[None] for s in st_s)
```
